```python
import math
import jax, jax.numpy as jnp
from jax import lax
import numpy as np

D_MODEL = 1024
BATCH = 1
SEQ = 16384
DEPTH = 2

CHUNK = 64
N_MIXERS = 2
EPS = 1e-6
MLA_HEADS = 8
QK_NOPE = 128
QK_ROPE = 64
V_HEAD = 128
Q_LORA = 384
KV_LORA = 256
ROPE_THETA = 10000.0
Q_BLOCK = 128
SSM_WIDTH = D_MODEL
SSM_GROUP = 16
SSM_GROUPS = SSM_WIDTH // SSM_GROUP
SSM_STATE = 64
DT_MIN = 1e-3
DT_MAX = 1e-1
D_FF = 2816
CONV_W = 3

N_MLA_LAYERS = (DEPTH + 1) // 2
N_SSM_LAYERS = DEPTH // 2

kernel_name = "hybrid_mla_s5_convffn_stream_encoder"


def rmsnorm(x, g):
    xf = x.astype(jnp.float32)
    y = xf * lax.rsqrt(jnp.mean(xf * xf, axis=-1, keepdims=True) + EPS)
    return (y * g.astype(jnp.float32)).astype(x.dtype)


def rope_tables(positions):
    inv = 1.0 / (ROPE_THETA ** (jnp.arange(0, QK_ROPE, 2, dtype=jnp.float32) / QK_ROPE))
    ang = positions.astype(jnp.float32)[..., None] * inv
    return jnp.cos(ang), jnp.sin(ang)


def apply_rope(x, cos, sin):
    xf = x.astype(jnp.float32)
    x1, x2 = jnp.split(xf, 2, axis=-1)
    return jnp.concatenate([x1 * cos - x2 * sin, x1 * sin + x2 * cos], axis=-1).astype(x.dtype)


def mla_mixer(h, cos, sin, w_a, g_q, g_kv, w_uq, w_ukv, w_o):
    B, S, _ = h.shape
    H = MLA_HEADS
    a = h @ w_a
    c_q, c_kv, k_rope = jnp.split(a, [Q_LORA, Q_LORA + KV_LORA], axis=-1)
    q = (rmsnorm(c_q, g_q) @ w_uq).reshape(B, S, H, QK_NOPE + QK_ROPE)
    q_nope = q[..., :QK_NOPE]
    q_rope = apply_rope(q[..., QK_NOPE:], cos[:, :, None, :], sin[:, :, None, :])
    k_rope = apply_rope(k_rope, cos, sin)
    kv = (rmsnorm(c_kv, g_kv) @ w_ukv).reshape(B, S, H, QK_NOPE + V_HEAD)
    k_nope = kv[..., :QK_NOPE]
    v = kv[..., QK_NOPE:]
    scale = (QK_NOPE + QK_ROPE) ** -0.5
    nb = S // Q_BLOCK
    key_chunk = jnp.arange(S) // CHUNK

    def block(args):
        qn, qr, i = args
        s = (jnp.einsum('bqhd,bkhd->bhqk', qn, k_nope)
             + jnp.einsum('bqhr,bkr->bhqk', qr, k_rope)).astype(jnp.float32) * scale
        q_chunk = (i * Q_BLOCK + jnp.arange(Q_BLOCK)) // CHUNK
        mask = key_chunk[None, :] <= q_chunk[:, None]
        s = jnp.where(mask[None, None], s, -jnp.inf)
        p = jax.nn.softmax(s, axis=-1).astype(v.dtype)
        return jnp.einsum('bhqk,bkhd->bqhd', p, v)

    qn_b = q_nope.reshape(B, nb, Q_BLOCK, H, QK_NOPE).transpose(1, 0, 2, 3, 4)
    qr_b = q_rope.reshape(B, nb, Q_BLOCK, H, QK_ROPE).transpose(1, 0, 2, 3, 4)
    o = lax.map(block, (qn_b, qr_b, jnp.arange(nb)))
    o = o.transpose(1, 0, 2, 3, 4).reshape(B, S, H * V_HEAD)
    return o @ w_o


def s5_mixer(h, w_in, lam_re, lam_im, log_dt, b_re, b_im, c_re, c_im, d_skip, w_glu):
    B, S, _ = h.shape
    G, P, C = SSM_GROUPS, SSM_STATE, SSM_GROUP
    u = (h @ w_in).astype(jnp.float32).reshape(B, S, G, C)
    dt = jnp.exp(log_dt.astype(jnp.float32))[:, None]
    lr = lam_re.astype(jnp.float32)
    li = lam_im.astype(jnp.float32)
    mag = jnp.exp(lr * dt)
    ar = mag * jnp.cos(li * dt)
    ai = mag * jnp.sin(li * dt)
    den = lr * lr + li * li
    nr = ar - 1.0
    coef_r = (nr * lr + ai * li) / den
    coef_i = (ai * lr - nr * li) / den
    br = b_re.astype(jnp.float32)
    bi = b_im.astype(jnp.float32)
    bbar_r = coef_r[..., None] * br - coef_i[..., None] * bi
    bbar_i = coef_r[..., None] * bi + coef_i[..., None] * br
    bu_r = jnp.einsum('bsgc,gpc->bsgp', u, bbar_r)
    bu_i = jnp.einsum('bsgc,gpc->bsgp', u, bbar_i)
    a_r = jnp.broadcast_to(ar, bu_r.shape)
    a_i = jnp.broadcast_to(ai, bu_i.shape)

    def combine(e1, e2):
        a1r, a1i, b1r, b1i = e1
        a2r, a2i, b2r, b2i = e2
        return (a2r * a1r - a2i * a1i,
                a2r * a1i + a2i * a1r,
                a2r * b1r - a2i * b1i + b2r,
                a2r * b1i + a2i * b1r + b2i)

    _, _, xr, xi = lax.associative_scan(combine, (a_r, a_i, bu_r, bu_i), axis=1)
    y = (jnp.einsum('bsgp,gcp->bsgc', xr, c_re.astype(jnp.float32))
         - jnp.einsum('bsgp,gcp->bsgc', xi, c_im.astype(jnp.float32)))
    y = y.reshape(B, S, SSM_WIDTH) + d_skip.astype(jnp.float32) * u.reshape(B, S, SSM_WIDTH)
    y = jax.nn.gelu(y).astype(h.dtype)
    val, gate = jnp.split(y @ w_glu, 2, axis=-1)
    return val * jax.nn.sigmoid(gate)


def conv_ffn(h, w_up, conv_w, conv_b, w_down):
    up = h @ w_up
    up = lax.conv_general_dilated(up, conv_w[:, None, :], window_strides=(1,),
                                  padding=[(CONV_W - 1, 0)],
                                  dimension_numbers=('NWC', 'WIO', 'NWC'),
                                  feature_group_count=2 * D_FF) + conv_b
    val, gate = jnp.split(up, 2, axis=-1)
    return (jax.nn.silu(gate) * val) @ w_down


def setup_inputs(seed: int = 0) -> dict:
    key = jax.random.key(seed)
    ks = iter(jax.random.split(key, 32))
    f32 = jnp.float32

    def nrm(shape, fan_in):
        return jax.random.normal(next(ks), shape, f32) * (fan_in ** -0.5)

    def gain(shape):
        return 1.0 + 0.02 * jax.random.normal(next(ks), shape, f32)

    Lm, Ls = N_MLA_LAYERS, N_SSM_LAYERS
    G, P, C = SSM_GROUPS, SSM_STATE, SSM_GROUP
    x = jax.random.normal(next(ks), (BATCH, SEQ, D_MODEL), f32)
    offset = jax.random.randint(next(ks), (BATCH, 1), 0, 4096, dtype=jnp.int32)
    positions = (offset + jnp.arange(SEQ, dtype=jnp.int32)[None, :]).astype(jnp.int32)

    mla_w_a = nrm((Lm, D_MODEL, Q_LORA + KV_LORA + QK_ROPE), D_MODEL)
    mla_g_q = gain((Lm, Q_LORA))
    mla_g_kv = gain((Lm, KV_LORA))
    mla_w_uq = nrm((Lm, Q_LORA, MLA_HEADS * (QK_NOPE + QK_ROPE)), Q_LORA)
    mla_w_ukv = nrm((Lm, KV_LORA, MLA_HEADS * (QK_NOPE + V_HEAD)), KV_LORA)
    mla_w_o = nrm((Lm, MLA_HEADS * V_HEAD, D_MODEL), MLA_HEADS * V_HEAD)

    ssm_w_in = nrm((Ls, D_MODEL, SSM_WIDTH), D_MODEL)
    ssm_lambda_re = -0.5 + 0.02 * jax.random.normal(next(ks), (Ls, G, P), f32)
    ssm_lambda_im = math.pi * jnp.arange(P, dtype=f32)[None, None, :] + 0.02 * jax.random.normal(next(ks), (Ls, G, P), f32)
    ssm_log_dt = jax.random.uniform(next(ks), (Ls, G), f32, math.log(DT_MIN), math.log(DT_MAX))
    ssm_b_re = nrm((Ls, G, P, C), 2 * C)
    ssm_b_im = nrm((Ls, G, P, C), 2 * C)
    ssm_c_re = nrm((Ls, G, C, P), 2 * P)
    ssm_c_im = nrm((Ls, G, C, P), 2 * P)
    ssm_d = jax.random.normal(next(ks), (Ls, SSM_WIDTH), f32)
    ssm_w_glu = nrm((Ls, SSM_WIDTH, 2 * D_MODEL), SSM_WIDTH)

    ffn_w_up = nrm((DEPTH, D_MODEL, 2 * D_FF), D_MODEL)
    ffn_conv_w = nrm((DEPTH, CONV_W, 2 * D_FF), CONV_W)
    ffn_conv_b = 0.02 * jax.random.normal(next(ks), (DEPTH, 2 * D_FF), f32)
    ffn_w_down = nrm((DEPTH, D_FF, D_MODEL), D_FF)

    g_mix = gain((DEPTH, D_MODEL))
    g_ffn = gain((DEPTH, D_MODEL))
    g_final = gain((D_MODEL,))
    return {"x": x, "positions": positions,
            "mla_w_a": mla_w_a, "mla_g_q": mla_g_q, "mla_g_kv": mla_g_kv,
            "mla_w_uq": mla_w_uq, "mla_w_ukv": mla_w_ukv, "mla_w_o": mla_w_o,
            "ssm_w_in": ssm_w_in, "ssm_lambda_re": ssm_lambda_re, "ssm_lambda_im": ssm_lambda_im,
            "ssm_log_dt": ssm_log_dt, "ssm_b_re": ssm_b_re, "ssm_b_im": ssm_b_im,
            "ssm_c_re": ssm_c_re, "ssm_c_im": ssm_c_im, "ssm_d": ssm_d, "ssm_w_glu": ssm_w_glu,
            "ffn_w_up": ffn_w_up, "ffn_conv_w": ffn_conv_w, "ffn_conv_b": ffn_conv_b,
            "ffn_w_down": ffn_w_down, "g_mix": g_mix, "g_ffn": g_ffn, "g_final": g_final}


def reference(x, positions, mla_w_a, mla_g_q, mla_g_kv, mla_w_uq, mla_w_ukv, mla_w_o,
              ssm_w_in, ssm_lambda_re, ssm_lambda_im, ssm_log_dt, ssm_b_re, ssm_b_im,
              ssm_c_re, ssm_c_im, ssm_d, ssm_w_glu, ffn_w_up, ffn_conv_w, ffn_conv_b,
              ffn_w_down, g_mix, g_ffn, g_final):
    cos, sin = rope_tables(positions)
    h = x
    for i in range(DEPTH):
        hn = rmsnorm(h, g_mix[i])
        j = i // N_MIXERS
        if i % N_MIXERS == 0:
            mix = mla_mixer(hn, cos, sin, mla_w_a[j], mla_g_q[j], mla_g_kv[j],
                            mla_w_uq[j], mla_w_ukv[j], mla_w_o[j])
        else:
            mix = s5_mixer(hn, ssm_w_in[j], ssm_lambda_re[j], ssm_lambda_im[j], ssm_log_dt[j],
                           ssm_b_re[j], ssm_b_im[j], ssm_c_re[j], ssm_c_im[j], ssm_d[j], ssm_w_glu[j])
        h = h + mix
        h = h + conv_ffn(rmsnorm(h, g_ffn[i]), ffn_w_up[i], ffn_conv_w[i], ffn_conv_b[i], ffn_w_down[i])
    return rmsnorm(h, g_final)
```

```python
import functools
import math

import jax
import jax.numpy as jnp
from jax import lax
from jax.experimental import pallas as pl
from jax.experimental.pallas import tpu as pltpu

F32 = jnp.float32
BF16 = jnp.bfloat16

D_MODEL = 1024
SEQ = 16384
EPS = 1e-6
CHUNK = 64
MLA_HEADS = 8
QK_NOPE = 128
QK_ROPE = 64
V_HEAD = 128
Q_LORA = 384
KV_LORA = 256
ROPE_THETA = 10000.0
SSM_GROUP = 16
SSM_GROUPS = 64
SSM_STATE = 64
D_FF = 2816
CONV_W = 3

LANES = 128
SUBLANES = 8
HEAD_PAD = 2 * LANES
VMEM_LIMIT = 56 * 1024 * 1024

PROJ_TS = 512
ATT_TQ = 512
ATT_TK = PROJ_TS
ROW_TS = 1024
FFN_TS = 512
FFN_HALO = 16
FFN_CF = 256
S5_TT = 1024
S5_ROWS = SUBLANES
S5_L = S5_TT // S5_ROWS
S5_BLK = 2 * LANES
S5_NBLK = D_MODEL // S5_BLK
S5_ST = S5_BLK // SSM_GROUP * SSM_STATE
NEG_BIG = -1e30


def _rms(x, g):
    return x * lax.rsqrt(jnp.mean(x * x, axis=-1, keepdims=True) + EPS) * g


def _const_spec(shape):
    nd = len(shape)
    return pl.BlockSpec(shape, lambda *_: (0,) * nd, pipeline_mode=pl.Buffered(1))


def _params(*sem):
    return pltpu.CompilerParams(dimension_semantics=sem, vmem_limit_bytes=VMEM_LIMIT)


def _mla_proj_kernel(x_ref, pos_ref, gmix_ref, wa_ref, gq_ref, gkv_ref, wq_ref,
                     wkt_ref, wv_ref, inv_ref, cmask_ref, sgn_ref,
                     q_ref, kt_ref, v_ref):
    hn = _rms(x_ref[...], gmix_ref[...])
    a = jnp.dot(hn.astype(BF16), wa_ref[...], preferred_element_type=F32)
    cq = _rms(a[:, :Q_LORA], gq_ref[...])
    ckv = _rms(a[:, Q_LORA:Q_LORA + KV_LORA], gkv_ref[...])

    ang = pos_ref[...].astype(F32) * inv_ref[...]
    ca = jnp.cos(ang) * cmask_ref[...]
    sa = jnp.sin(ang) * sgn_ref[...]

    def rope(v2):
        return v2 * ca + pltpu.roll(v2, LANES // 2, 1) * sa

    scale = (QK_NOPE + QK_ROPE) ** -0.5
    qraw = jnp.dot(cq.astype(BF16), wq_ref[...], preferred_element_type=F32)
    for h in range(MLA_HEADS):
        base = h * HEAD_PAD
        q_ref[h, :, 0:LANES] = (qraw[:, base:base + LANES] * scale).astype(BF16)
        q_ref[h, :, LANES:HEAD_PAD] = (
            rope(qraw[:, base + LANES:base + HEAD_PAD]) * scale).astype(BF16)

    ckv_b = ckv.astype(BF16)
    ckv_t = ckv.T.astype(BF16)
    kn_t = jnp.dot(wkt_ref[...], ckv_t, preferred_element_type=F32)
    kr_t = rope(a[:, Q_LORA + KV_LORA:]).T.astype(BF16)
    v = jnp.dot(ckv_b, wv_ref[...], preferred_element_type=F32)
    for h in range(MLA_HEADS):
        kt_ref[h, 0, 0:LANES, :] = kn_t[h * LANES:(h + 1) * LANES, :].astype(BF16)
        kt_ref[h, 0, LANES:HEAD_PAD, :] = kr_t
        v_ref[h] = v[:, h * V_HEAD:(h + 1) * V_HEAD].astype(BF16)


def _mla_proj(x, pos, g_mix, w_a, g_q, g_kv, w_uq, w_ukv):
    S = x.shape[0]
    H = MLA_HEADS
    ts = PROJ_TS
    kr = w_a[:, Q_LORA + KV_LORA:]
    kr_sw = jnp.concatenate([kr[:, QK_ROPE // 2:], kr[:, :QK_ROPE // 2]], axis=1)
    wa_ext = jnp.concatenate([w_a, kr_sw], axis=1).astype(BF16)
    wq = w_uq.reshape(Q_LORA, H, QK_NOPE + QK_ROPE)
    wq_r = wq[:, :, QK_NOPE:]
    wq_sw = jnp.concatenate([wq_r[:, :, QK_ROPE // 2:], wq_r[:, :, :QK_ROPE // 2]], axis=2)
    wq_ext = jnp.concatenate([wq, wq_sw], axis=2).reshape(Q_LORA, H * HEAD_PAD).astype(BF16)
    wkv = w_ukv.reshape(KV_LORA, H, QK_NOPE + V_HEAD)
    wk_t = wkv[:, :, :QK_NOPE].reshape(KV_LORA, H * QK_NOPE).T.astype(BF16)
    wv = wkv[:, :, QK_NOPE:].reshape(KV_LORA, H * V_HEAD).astype(BF16)
    half = QK_ROPE // 2
    inv = 1.0 / (ROPE_THETA ** (jnp.arange(0, QK_ROPE, 2, dtype=F32) / QK_ROPE))
    zeros = jnp.zeros((LANES - QK_ROPE,), F32)
    inv_row = jnp.concatenate([inv, inv, zeros])[None, :]
    cmask = jnp.concatenate([jnp.ones((QK_ROPE,), F32), zeros])[None, :]
    sgn = jnp.concatenate([-jnp.ones((half,), F32), jnp.ones((half,), F32), zeros])[None, :]

    n = S // ts
    row = lambda i: (i, 0)
    return pl.pallas_call(
        _mla_proj_kernel,
        grid=(n,),
        in_specs=[
            pl.BlockSpec((ts, D_MODEL), row),
            pl.BlockSpec((ts, 1), row),
            _const_spec((1, D_MODEL)),
            _const_spec(wa_ext.shape),
            _const_spec((1, Q_LORA)),
            _const_spec((1, KV_LORA)),
            _const_spec(wq_ext.shape),
            _const_spec(wk_t.shape),
            _const_spec(wv.shape),
            _const_spec((1, LANES)),
            _const_spec((1, LANES)),
            _const_spec((1, LANES)),
        ],
        out_specs=[
            pl.BlockSpec((H, ts, HEAD_PAD), lambda i: (0, i, 0)),
            pl.BlockSpec((H, 1, HEAD_PAD, ts), lambda i: (0, i, 0, 0)),
            pl.BlockSpec((H, ts, V_HEAD), lambda i: (0, i, 0)),
        ],
        out_shape=[
            jax.ShapeDtypeStruct((H, S, HEAD_PAD), BF16),
            jax.ShapeDtypeStruct((H, n, HEAD_PAD, ts), BF16),
            jax.ShapeDtypeStruct((H, S, V_HEAD), BF16),
        ],
        compiler_params=_params("parallel"),
        name="mla_proj",
    )(x, pos, g_mix[None, :], wa_ext, g_q[None, :], g_kv[None, :], wq_ext, wk_t, wv,
      inv_row, cmask, sgn)


def _attn_kernel(q_ref, kt_ref, v_ref, o_ref, m_ref, l_ref, acc_ref):
    i = pl.program_id(1)
    q = q_ref[0]
    m_ref[...] = jnp.full(m_ref.shape, NEG_BIG, F32)
    l_ref[...] = jnp.zeros(l_ref.shape, F32)
    acc_ref[...] = jnp.zeros(acc_ref.shape, F32)

    def step(j, masked):
        s = jnp.dot(q, kt_ref[0, j], preferred_element_type=F32)
        if masked:
            rq = lax.broadcasted_iota(jnp.int32, s.shape, 0) // CHUNK
            ck = lax.broadcasted_iota(jnp.int32, s.shape, 1) // CHUNK
            s = jnp.where(ck <= rq, s, NEG_BIG)
        m_prev = m_ref[...]
        m_new = jnp.maximum(m_prev, jnp.max(s, axis=1, keepdims=True))
        alpha = jnp.exp(m_prev - m_new)
        p = jnp.exp(s - pltpu.repeat(m_new, ATT_TK // LANES, 1))
        l_ref[...] = alpha * l_ref[...] + jnp.sum(p, axis=1, keepdims=True)
        ks = pl.multiple_of(j * ATT_TK, ATT_TK)
        pv = jnp.dot(p.astype(BF16), v_ref[0, pl.ds(ks, ATT_TK), :],
                     preferred_element_type=F32)
        acc_ref[...] = acc_ref[...] * alpha + pv
        m_ref[...] = m_new

    def body(j, carry):
        step(j, False)
        return carry

    lax.fori_loop(0, i, body, 0)
    step(i, True)
    o_ref[...] = (acc_ref[...] / l_ref[...]).astype(o_ref.dtype)


def _attention(q, kt, v):
    H, S, _ = q.shape
    nq = S // ATT_TQ
    nk = S // ATT_TK
    return pl.pallas_call(
        _attn_kernel,
        grid=(H, nq),
        in_specs=[
            pl.BlockSpec((1, ATT_TQ, HEAD_PAD), lambda h, i: (h, i, 0)),
            pl.BlockSpec((1, nk, HEAD_PAD, ATT_TK), lambda h, i: (h, 0, 0, 0)),
            pl.BlockSpec((1, S, V_HEAD), lambda h, i: (h, 0, 0)),
        ],
        out_specs=pl.BlockSpec((ATT_TQ, V_HEAD), lambda h, i: (i, h)),
        out_shape=jax.ShapeDtypeStruct((S, H * V_HEAD), BF16),
        scratch_shapes=[
            pltpu.VMEM((ATT_TQ, LANES), F32),
            pltpu.VMEM((ATT_TQ, LANES), F32),
            pltpu.VMEM((ATT_TQ, V_HEAD), F32),
        ],
        compiler_params=_params("parallel", "arbitrary"),
        name="mla_attention",
    )(q, kt, v)


def _out_proj_kernel(h_ref, o_ref, w_ref, out_ref):
    out_ref[...] = h_ref[...] + jnp.dot(o_ref[...], w_ref[...], preferred_element_type=F32)


def _out_proj(h, o, w):
    S = h.shape[0]
    ts = ROW_TS
    row = lambda i: (i, 0)
    return pl.pallas_call(
        _out_proj_kernel,
        grid=(S // ts,),
        in_specs=[pl.BlockSpec((ts, D_MODEL), row),
                  pl.BlockSpec((ts, o.shape[1]), row),
                  _const_spec(w.shape)],
        out_specs=pl.BlockSpec((ts, D_MODEL), row),
        out_shape=jax.ShapeDtypeStruct((S, D_MODEL), F32),
        compiler_params=_params("parallel"),
        name="attn_out_proj",
    )(h, o, w.astype(BF16))


def _norm_proj_kernel(h_ref, g_ref, w_ref, out_ref):
    hn = _rms(h_ref[...], g_ref[...])
    out_ref[...] = jnp.dot(hn.astype(BF16), w_ref[...], preferred_element_type=F32)


def _norm_proj(h, g, w):
    S = h.shape[0]
    ts = ROW_TS
    row = lambda i: (i, 0)
    return pl.pallas_call(
        _norm_proj_kernel,
        grid=(S // ts,),
        in_specs=[pl.BlockSpec((ts, D_MODEL), row),
                  _const_spec((1, D_MODEL)),
                  _const_spec(w.shape)],
        out_specs=pl.BlockSpec((ts, w.shape[1]), row),
        out_shape=jax.ShapeDtypeStruct((S, w.shape[1]), F32),
        compiler_params=_params("parallel"),
        name="norm_proj",
    )(h, g[None, :], w.astype(BF16))


def _ffn_kernel(h_ref, halo_ref, g_ref, wup_ref, cw_ref, cb_ref, wdn_ref, gfin_ref,
                out_ref, act_ref, *, final_norm):
    i = pl.program_id(0)
    ts = h_ref.shape[0]
    h = h_ref[...]
    g = g_ref[...]
    hn = _rms(h, g)
    halo = _rms(halo_ref[...], g) * (i > 0).astype(F32)
    hn_ext = jnp.concatenate([halo, hn], axis=0).astype(BF16)

    def conv(up, col):
        w = cw_ref[:, col:col + FFN_CF]
        b = cb_ref[:, col:col + FFN_CF]
        return (up[FFN_HALO - 2:FFN_HALO - 2 + ts] * w[0:1]
                + up[FFN_HALO - 1:FFN_HALO - 1 + ts] * w[1:2]
                + up[FFN_HALO:FFN_HALO + ts] * w[2:3] + b)

    for c in range(D_FF // FFN_CF):
        cv = c * FFN_CF
        cg = D_FF + cv
        up_v = jnp.dot(hn_ext, wup_ref[:, cv:cv + FFN_CF], preferred_element_type=F32)
        up_g = jnp.dot(hn_ext, wup_ref[:, cg:cg + FFN_CF], preferred_element_type=F32)
        val = conv(up_v, cv)
        gate = conv(up_g, cg)
        act_ref[:, cv:cv + FFN_CF] = (gate * jax.nn.sigmoid(gate) * val).astype(BF16)

    out = h + jnp.dot(act_ref[...], wdn_ref[...], preferred_element_type=F32)
    if final_norm:
        out = _rms(out, gfin_ref[...])
    out_ref[...] = out


def _conv_ffn(h, g, w_up, conv_w, conv_b, w_down, g_final, final_norm):
    S = h.shape[0]
    ts = FFN_TS
    hb = ts // FFN_HALO
    row = lambda i: (i, 0)
    return pl.pallas_call(
        functools.partial(_ffn_kernel, final_norm=final_norm),
        grid=(S // ts,),
        in_specs=[
            pl.BlockSpec((ts, D_MODEL), row),
            pl.BlockSpec((FFN_HALO, D_MODEL), lambda i: (jnp.maximum(i * hb - 1, 0), 0)),
            _const_spec((1, D_MODEL)),
            _const_spec(w_up.shape),
            _const_spec(conv_w.shape),
            _const_spec((1, 2 * D_FF)),
            _const_spec(w_down.shape),
            _const_spec((1, D_MODEL)),
        ],
        out_specs=pl.BlockSpec((ts, D_MODEL), row),
        out_shape=jax.ShapeDtypeStruct((S, D_MODEL), F32),
        scratch_shapes=[pltpu.VMEM((ts, D_FF), BF16)],
        compiler_params=_params("parallel"),
        name="conv_ffn",
    )(h, h, g[None, :], w_up.astype(BF16), conv_w, conv_b[None, :],
      w_down.astype(BF16), g_final[None, :])


def _s5_kernel(u_ref, bb_ref, cc_ref, a_ref, lp_ref, w_ref, y_ref, buf_ref, carry_ref):
    t = pl.program_id(1)
    nc = S5_ST // LANES

    @pl.when(t == 0)
    def _():
        carry_ref[...] = jnp.zeros(carry_ref.shape, F32)

    bu = jnp.dot(u_ref[...].astype(BF16), bb_ref[0], preferred_element_type=F32)
    for c in range(2 * nc):
        buf_ref[c] = bu[:, c * LANES:(c + 1) * LANES]

    def col(ref_row, c):
        return jnp.broadcast_to(ref_row[:, c * LANES:(c + 1) * LANES], (S5_ROWS, LANES))

    ar = [col(a_ref[0, 0:1, :], c) for c in range(nc)]
    ai = [col(a_ref[0, 1:2, :], c) for c in range(nc)]

    def rows(l):
        return pl.ds(l, S5_ROWS, stride=S5_L)

    def pass1(l, x):
        out = []
        for c in range(nc):
            xr, xi = x[2 * c], x[2 * c + 1]
            nr = ar[c] * xr - ai[c] * xi + buf_ref[c, rows(l), :]
            ni = ar[c] * xi + ai[c] * xr + buf_ref[nc + c, rows(l), :]
            buf_ref[c, rows(l), :] = nr
            buf_ref[nc + c, rows(l), :] = ni
            out += [nr, ni]
        return tuple(out)

    zero = jnp.zeros((S5_ROWS, LANES), F32)
    ends = lax.fori_loop(0, S5_L, pass1, (zero,) * (2 * nc))

    sub = lax.broadcasted_iota(jnp.int32, (S5_ROWS, LANES), 0)
    prev = []
    for c in range(nc):
        sl = slice(c * LANES, (c + 1) * LANES)
        er, ei = ends[2 * c], ends[2 * c + 1]
        d = 1
        k = 0
        while d < S5_ROWS:
            pr = lp_ref[0, k, 0:1, sl]
            pi = lp_ref[0, k, 1:2, sl]
            sr = jnp.where(sub >= d, pltpu.roll(er, d, 0), 0.0)
            si = jnp.where(sub >= d, pltpu.roll(ei, d, 0), 0.0)
            er, ei = er + pr * sr - pi * si, ei + pr * si + pi * sr
            d *= 2
            k += 1
        cr = carry_ref[0:1, sl]
        ci = carry_ref[1:2, sl]
        wr = w_ref[0, 0, :, sl]
        wi = w_ref[0, 1, :, sl]
        er, ei = er + wr * cr - wi * ci, ei + wr * ci + wi * cr
        carry_ref[0:1, sl] = er[S5_ROWS - 1:S5_ROWS, :]
        carry_ref[1:2, sl] = ei[S5_ROWS - 1:S5_ROWS, :]
        prev += [jnp.where(sub >= 1, pltpu.roll(er, 1, 0), jnp.broadcast_to(cr, (S5_ROWS, LANES))),
                 jnp.where(sub >= 1, pltpu.roll(ei, 1, 0), jnp.broadcast_to(ci, (S5_ROWS, LANES)))]

    def pass2(l, p):
        out = []
        for c in range(nc):
            pr_, pi_ = p[2 * c], p[2 * c + 1]
            nr = ar[c] * pr_ - ai[c] * pi_
            ni = ar[c] * pi_ + ai[c] * pr_
            buf_ref[c, rows(l), :] = buf_ref[c, rows(l), :] + nr
            buf_ref[nc + c, rows(l), :] = buf_ref[nc + c, rows(l), :] + ni
            out += [nr, ni]
        return tuple(out)

    lax.fori_loop(0, S5_L, pass2, tuple(prev))
    x = jnp.concatenate([buf_ref[c].astype(BF16) for c in range(2 * nc)], axis=1)
    y_ref[...] = jnp.dot(x, cc_ref[0], preferred_element_type=F32)


def _s5_tables(lam_re, lam_im, log_dt, b_re, b_im, c_re, c_im):
    G, P, C = SSM_GROUPS, SSM_STATE, SSM_GROUP
    nb, gb = S5_NBLK, SSM_GROUPS // S5_NBLK
    dt = jnp.exp(log_dt.astype(F32))[:, None]
    lr = lam_re.astype(F32)
    li = lam_im.astype(F32)
    mag = jnp.exp(lr * dt)
    ar = mag * jnp.cos(li * dt)
    ai = mag * jnp.sin(li * dt)
    den = lr * lr + li * li
    nr = ar - 1.0
    coef_r = (nr * lr + ai * li) / den
    coef_i = (ai * lr - nr * li) / den
    br = b_re.astype(F32)
    bi = b_im.astype(F32)
    bbar_r = coef_r[..., None] * br - coef_i[..., None] * bi
    bbar_i = coef_r[..., None] * bi + coef_i[..., None] * br
    eye = jnp.eye(gb, dtype=F32)

    def blockdiag_in(m):
        return jnp.einsum('bgpc,gh->bgchp', m.reshape(nb, gb, P, C), eye).reshape(nb, gb * C, gb * P)

    def blockdiag_out(m):
        return jnp.einsum('bgcp,gh->bgphc', m.reshape(nb, gb, C, P), eye).reshape(nb, gb * P, gb * C)

    bb = jnp.concatenate([blockdiag_in(bbar_r), blockdiag_in(bbar_i)], axis=2).astype(BF16)
    cc = jnp.concatenate([blockdiag_out(c_re.astype(F32)),
                          -blockdiag_out(c_im.astype(F32))], axis=1).astype(BF16)

    def apow(nsteps):
        m = jnp.exp(lr * dt * nsteps)
        return m * jnp.cos(li * dt * nsteps), m * jnp.sin(li * dt * nsteps)

    def blk(x):
        lead = x.shape[:-2]
        x = x.reshape(lead + (nb, gb * P))
        return jnp.moveaxis(x, -2, 0)

    a_tab = jnp.stack([blk(ar), blk(ai)], axis=1)
    lp = []
    d = 1
    while d < S5_ROWS:
        pr, pi = apow(float(S5_L * d))
        lp.append(jnp.stack([blk(pr), blk(pi)], axis=1))
        d *= 2
    lp_tab = jnp.stack(lp, axis=1)
    steps = (S5_L * (jnp.arange(S5_ROWS, dtype=F32) + 1.0))[:, None, None]
    wr, wi = apow(steps)
    w_tab = jnp.stack([blk(wr), blk(wi)], axis=1)
    return bb, cc, a_tab, lp_tab, w_tab


def _s5_core(u, bb, cc, a_tab, lp_tab, w_tab):
    S = u.shape[0]
    nt = S // S5_TT
    return pl.pallas_call(
        _s5_kernel,
        grid=(S5_NBLK, nt),
        in_specs=[
            pl.BlockSpec((S5_TT, S5_BLK), lambda b, t: (t, b)),
            pl.BlockSpec((1, S5_BLK, 2 * S5_ST), lambda b, t: (b, 0, 0)),
            pl.BlockSpec((1, 2 * S5_ST, S5_BLK), lambda b, t: (b, 0, 0)),
            pl.BlockSpec((1, 2, S5_ST), lambda b, t: (b, 0, 0)),
            pl.BlockSpec((1,) + lp_tab.shape[1:], lambda b, t: (b, 0, 0, 0)),
            pl.BlockSpec((1, 2, S5_ROWS, S5_ST), lambda b, t: (b, 0, 0, 0)),
        ],
        out_specs=pl.BlockSpec((S5_TT, S5_BLK), lambda b, t: (t, b)),
        out_shape=jax.ShapeDtypeStruct((S, D_MODEL), F32),
        scratch_shapes=[pltpu.VMEM((2 * S5_ST // LANES, S5_TT, LANES), F32),
                        pltpu.VMEM((2, S5_ST), F32)],
        compiler_params=_params("parallel", "arbitrary"),
        name="s5_core",
    )(u, bb, cc, a_tab, lp_tab, w_tab)


def _s5_glu_kernel(h_ref, y_ref, u_ref, d_ref, w_ref, out_ref):
    y = jax.nn.gelu(y_ref[...] + d_ref[...] * u_ref[...])
    z = jnp.dot(y.astype(BF16), w_ref[...], preferred_element_type=F32)
    out_ref[...] = h_ref[...] + z[:, :D_MODEL] * jax.nn.sigmoid(z[:, D_MODEL:])


def _s5_glu(h, y, u, d, w):
    S = h.shape[0]
    ts = ROW_TS
    row = lambda i: (i, 0)
    tile = pl.BlockSpec((ts, D_MODEL), row)
    return pl.pallas_call(
        _s5_glu_kernel,
        grid=(S // ts,),
        in_specs=[tile, tile, tile, _const_spec((1, D_MODEL)), _const_spec(w.shape)],
        out_specs=tile,
        out_shape=jax.ShapeDtypeStruct((S, D_MODEL), F32),
        compiler_params=_params("parallel"),
        name="s5_glu",
    )(h, y, u, d[None, :], w.astype(BF16))


def kernel(x, positions, mla_w_a, mla_g_q, mla_g_kv, mla_w_uq, mla_w_ukv, mla_w_o,
           ssm_w_in, ssm_lambda_re, ssm_lambda_im, ssm_log_dt, ssm_b_re, ssm_b_im,
           ssm_c_re, ssm_c_im, ssm_d, ssm_w_glu, ffn_w_up, ffn_conv_w, ffn_conv_b,
           ffn_w_down, g_mix, g_ffn, g_final):
    B, S, D = x.shape
    assert (B, S, D) == (1, SEQ, D_MODEL)
    h = x.reshape(S, D)
    pos = positions.reshape(S, 1)

    q, kt, v = _mla_proj(h, pos, g_mix[0], mla_w_a[0], mla_g_q[0], mla_g_kv[0],
                         mla_w_uq[0], mla_w_ukv[0])
    o = _attention(q, kt, v)
    h = _out_proj(h, o, mla_w_o[0])
    h = _conv_ffn(h, g_ffn[0], ffn_w_up[0], ffn_conv_w[0], ffn_conv_b[0], ffn_w_down[0],
                  g_final, final_norm=False)

    u = _norm_proj(h, g_mix[1], ssm_w_in[0])
    tabs = _s5_tables(ssm_lambda_re[0], ssm_lambda_im[0], ssm_log_dt[0],
                      ssm_b_re[0], ssm_b_im[0], ssm_c_re[0], ssm_c_im[0])
    y = _s5_core(u, *tabs)
    h = _s5_glu(h, y, u, ssm_d[0], ssm_w_glu[0])
    h = _conv_ffn(h, g_ffn[1], ffn_w_up[1], ffn_conv_w[1], ffn_conv_b[1], ffn_w_down[1],
                  g_final, final_norm=True)
    return h.reshape(B, S, D)
```

```python
import functools
import math

import jax
import jax.numpy as jnp
from jax import lax
from jax.experimental import pallas as pl
from jax.experimental.pallas import tpu as pltpu

F32 = jnp.float32
BF16 = jnp.bfloat16

D_MODEL = 1024
SEQ = 16384
EPS = 1e-6
CHUNK = 64
MLA_HEADS = 8
QK_NOPE = 128
QK_ROPE = 64
V_HEAD = 128
Q_LORA = 384
KV_LORA = 256
ROPE_THETA = 10000.0
SSM_GROUP = 16
SSM_GROUPS = 64
SSM_STATE = 64
D_FF = 2816
CONV_W = 3

LANES = 128
SUBLANES = 8
HEAD_PAD = 2 * LANES
VMEM_LIMIT = 56 * 1024 * 1024

PROJ_TS = 512
ATT_TQ = 512
ATT_TK = PROJ_TS
ATT_SUB = 512
ROW_TS = 1024
FFN_TS = 512
FFN_HALO = 16
FFN_CF = 256
S5_TT = 1024
S5_ROWS = SUBLANES
S5_L = S5_TT // S5_ROWS
S5_BLK = 2 * LANES
S5_NBLK = D_MODEL // S5_BLK
S5_ST = S5_BLK // SSM_GROUP * SSM_STATE
NEG_BIG = -1e30


def _rms(x, g):
    return x * lax.rsqrt(jnp.mean(x * x, axis=-1, keepdims=True) + EPS) * g


def _const_spec(shape):
    nd = len(shape)
    return pl.BlockSpec(shape, lambda *_: (0,) * nd, pipeline_mode=pl.Buffered(1))


def _params(*sem):
    return pltpu.CompilerParams(dimension_semantics=sem, vmem_limit_bytes=VMEM_LIMIT)


def _mla_proj_kernel(x_ref, pos_ref, gmix_ref, wa_ref, gq_ref, gkv_ref, wq_ref,
                     wkt_ref, wv_ref, inv_ref, cmask_ref, sgn_ref,
                     q_ref, kt_ref, v_ref):
    hn = _rms(x_ref[...], gmix_ref[...])
    a = jnp.dot(hn.astype(BF16), wa_ref[...], preferred_element_type=F32)
    cq = _rms(a[:, :Q_LORA], gq_ref[...])
    ckv = _rms(a[:, Q_LORA:Q_LORA + KV_LORA], gkv_ref[...])

    ang = pos_ref[...].astype(F32) * inv_ref[...]
    ca = jnp.cos(ang) * cmask_ref[...]
    sa = jnp.sin(ang) * sgn_ref[...]

    def rope(v2):
        return v2 * ca + pltpu.roll(v2, LANES // 2, 1) * sa

    scale = (QK_NOPE + QK_ROPE) ** -0.5 * math.log2(math.e)
    qraw = jnp.dot(cq.astype(BF16), wq_ref[...], preferred_element_type=F32)
    for h in range(MLA_HEADS):
        base = h * HEAD_PAD
        q_ref[h, :, 0:LANES] = (qraw[:, base:base + LANES] * scale).astype(BF16)
        q_ref[h, :, LANES:HEAD_PAD] = (
            rope(qraw[:, base + LANES:base + HEAD_PAD]) * scale).astype(BF16)

    ckv_b = ckv.astype(BF16)
    ckv_t = ckv.T.astype(BF16)
    kn_t = jnp.dot(wkt_ref[...], ckv_t, preferred_element_type=F32)
    kr_t = rope(a[:, Q_LORA + KV_LORA:]).T.astype(BF16)
    v = jnp.dot(ckv_b, wv_ref[...], preferred_element_type=F32)
    for h in range(MLA_HEADS):
        kt_ref[h, 0, 0:LANES, :] = kn_t[h * LANES:(h + 1) * LANES, :].astype(BF16)
        kt_ref[h, 0, LANES:HEAD_PAD, :] = kr_t
        v_ref[h] = v[:, h * V_HEAD:(h + 1) * V_HEAD].astype(BF16)


def _mla_proj(x, pos, g_mix, w_a, g_q, g_kv, w_uq, w_ukv):
    S = x.shape[0]
    H = MLA_HEADS
    ts = PROJ_TS
    kr = w_a[:, Q_LORA + KV_LORA:]
    kr_sw = jnp.concatenate([kr[:, QK_ROPE // 2:], kr[:, :QK_ROPE // 2]], axis=1)
    wa_ext = jnp.concatenate([w_a, kr_sw], axis=1).astype(BF16)
    wq = w_uq.reshape(Q_LORA, H, QK_NOPE + QK_ROPE)
    wq_r = wq[:, :, QK_NOPE:]
    wq_sw = jnp.concatenate([wq_r[:, :, QK_ROPE // 2:], wq_r[:, :, :QK_ROPE // 2]], axis=2)
    wq_ext = jnp.concatenate([wq, wq_sw], axis=2).reshape(Q_LORA, H * HEAD_PAD).astype(BF16)
    wkv = w_ukv.reshape(KV_LORA, H, QK_NOPE + V_HEAD)
    wk_t = wkv[:, :, :QK_NOPE].reshape(KV_LORA, H * QK_NOPE).T.astype(BF16)
    wv = wkv[:, :, QK_NOPE:].reshape(KV_LORA, H * V_HEAD).astype(BF16)
    half = QK_ROPE // 2
    inv = 1.0 / (ROPE_THETA ** (jnp.arange(0, QK_ROPE, 2, dtype=F32) / QK_ROPE))
    zeros = jnp.zeros((LANES - QK_ROPE,), F32)
    inv_row = jnp.concatenate([inv, inv, zeros])[None, :]
    cmask = jnp.concatenate([jnp.ones((QK_ROPE,), F32), zeros])[None, :]
    sgn = jnp.concatenate([-jnp.ones((half,), F32), jnp.ones((half,), F32), zeros])[None, :]

    n = S // ts
    row = lambda i: (i, 0)
    return pl.pallas_call(
        _mla_proj_kernel,
        grid=(n,),
        in_specs=[
            pl.BlockSpec((ts, D_MODEL), row),
            pl.BlockSpec((ts, 1), row),
            _const_spec((1, D_MODEL)),
            _const_spec(wa_ext.shape),
            _const_spec((1, Q_LORA)),
            _const_spec((1, KV_LORA)),
            _const_spec(wq_ext.shape),
            _const_spec(wk_t.shape),
            _const_spec(wv.shape),
            _const_spec((1, LANES)),
            _const_spec((1, LANES)),
            _const_spec((1, LANES)),
        ],
        out_specs=[
            pl.BlockSpec((H, ts, HEAD_PAD), lambda i: (0, i, 0)),
            pl.BlockSpec((H, 1, HEAD_PAD, ts), lambda i: (0, i, 0, 0)),
            pl.BlockSpec((H, ts, V_HEAD), lambda i: (0, i, 0)),
        ],
        out_shape=[
            jax.ShapeDtypeStruct((H, S, HEAD_PAD), BF16),
            jax.ShapeDtypeStruct((H, n, HEAD_PAD, ts), BF16),
            jax.ShapeDtypeStruct((H, S, V_HEAD), BF16),
        ],
        compiler_params=_params("parallel"),
        name="mla_proj",
    )(x, pos, g_mix[None, :], wa_ext, g_q[None, :], g_kv[None, :], wq_ext, wk_t, wv,
      inv_row, cmask, sgn)


def _attn_kernel(q_ref, kt_ref, v_ref, o_ref, m_ref, l_ref, acc_ref):
    i = pl.program_id(1)
    m_ref[...] = jnp.full(m_ref.shape, NEG_BIG, F32)
    l_ref[...] = jnp.zeros(l_ref.shape, F32)
    acc_ref[...] = jnp.zeros(acc_ref.shape, F32)

    def sub_step(j, rows, masked):
        r0 = rows.start
        s = jnp.dot(q_ref[0, rows, :], kt_ref[0, j], preferred_element_type=F32)
        if masked:
            rq = (r0 + lax.broadcasted_iota(jnp.int32, s.shape, 0)) // CHUNK
            ck = lax.broadcasted_iota(jnp.int32, s.shape, 1) // CHUNK
            s = jnp.where(ck <= rq, s, NEG_BIG)
        cols = [s[:, c * LANES:(c + 1) * LANES] for c in range(ATT_TK // LANES)]
        m_prev = m_ref[rows, :]
        m_new = jnp.maximum(m_prev, jnp.max(functools.reduce(jnp.maximum, cols),
                                            axis=1, keepdims=True))
        alpha = jnp.exp2(m_prev - m_new)
        ps = [jnp.exp2(c - m_new) for c in cols]
        l_ref[rows, :] = alpha * l_ref[rows, :] + functools.reduce(jnp.add, ps)
        p = jnp.concatenate(ps, axis=1)
        ks = pl.multiple_of(j * ATT_TK, ATT_TK)
        pv = jnp.dot(p.astype(BF16), v_ref[0, pl.ds(ks, ATT_TK), :],
                     preferred_element_type=F32)
        acc_ref[rows, :] = acc_ref[rows, :] * alpha + pv
        m_ref[rows, :] = m_new

    def step(j, masked):
        for r0 in range(0, ATT_TQ, ATT_SUB):
            sub_step(j, slice(r0, r0 + ATT_SUB), masked)

    def body(j, carry):
        step(j, False)
        return carry

    lax.fori_loop(0, i, body, 0)
    step(i, True)
    l = jnp.sum(l_ref[...], axis=1, keepdims=True)
    o_ref[...] = (acc_ref[...] / l).astype(o_ref.dtype)


def _attention(q, kt, v):
    H, S, _ = q.shape
    nq = S // ATT_TQ
    nk = S // ATT_TK
    return pl.pallas_call(
        _attn_kernel,
        grid=(H, nq),
        in_specs=[
            pl.BlockSpec((1, ATT_TQ, HEAD_PAD), lambda h, i: (h, i, 0)),
            pl.BlockSpec((1, nk, HEAD_PAD, ATT_TK), lambda h, i: (h, 0, 0, 0)),
            pl.BlockSpec((1, S, V_HEAD), lambda h, i: (h, 0, 0)),
        ],
        out_specs=pl.BlockSpec((ATT_TQ, V_HEAD), lambda h, i: (i, h)),
        out_shape=jax.ShapeDtypeStruct((S, H * V_HEAD), BF16),
        scratch_shapes=[
            pltpu.VMEM((ATT_TQ, LANES), F32),
            pltpu.VMEM((ATT_TQ, LANES), F32),
            pltpu.VMEM((ATT_TQ, V_HEAD), F32),
        ],
        compiler_params=_params("parallel", "arbitrary"),
        name="mla_attention",
    )(q, kt, v)


def _out_proj_kernel(h_ref, o_ref, w_ref, out_ref):
    out_ref[...] = h_ref[...] + jnp.dot(o_ref[...], w_ref[...], preferred_element_type=F32)


def _out_proj(h, o, w):
    S = h.shape[0]
    ts = ROW_TS
    row = lambda i: (i, 0)
    return pl.pallas_call(
        _out_proj_kernel,
        grid=(S // ts,),
        in_specs=[pl.BlockSpec((ts, D_MODEL), row),
                  pl.BlockSpec((ts, o.shape[1]), row),
                  _const_spec(w.shape)],
        out_specs=pl.BlockSpec((ts, D_MODEL), row),
        out_shape=jax.ShapeDtypeStruct((S, D_MODEL), F32),
        compiler_params=_params("parallel"),
        name="attn_out_proj",
    )(h, o, w.astype(BF16))


def _norm_proj_kernel(h_ref, g_ref, w_ref, out_ref):
    hn = _rms(h_ref[...], g_ref[...])
    out_ref[...] = jnp.dot(hn.astype(BF16), w_ref[...], preferred_element_type=F32)


def _norm_proj(h, g, w):
    S = h.shape[0]
    ts = ROW_TS
    row = lambda i: (i, 0)
    return pl.pallas_call(
        _norm_proj_kernel,
        grid=(S // ts,),
        in_specs=[pl.BlockSpec((ts, D_MODEL), row),
                  _const_spec((1, D_MODEL)),
                  _const_spec(w.shape)],
        out_specs=pl.BlockSpec((ts, w.shape[1]), row),
        out_shape=jax.ShapeDtypeStruct((S, w.shape[1]), F32),
        compiler_params=_params("parallel"),
        name="norm_proj",
    )(h, g[None, :], w.astype(BF16))


def _ffn_kernel(h_ref, halo_ref, g_ref, wup_ref, cw_ref, cb_ref, wdn_ref, gfin_ref,
                out_ref, act_ref, *, final_norm):
    i = pl.program_id(0)
    ts = h_ref.shape[0]
    h = h_ref[...]
    g = g_ref[...]
    hn = _rms(h, g)
    halo = _rms(halo_ref[...], g) * (i > 0).astype(F32)
    hn_ext = jnp.concatenate([halo, hn], axis=0).astype(BF16)

    def conv(up, col):
        w = cw_ref[:, col:col + FFN_CF]
        b = cb_ref[:, col:col + FFN_CF]
        return (up[FFN_HALO - 2:FFN_HALO - 2 + ts] * w[0:1]
                + up[FFN_HALO - 1:FFN_HALO - 1 + ts] * w[1:2]
                + up[FFN_HALO:FFN_HALO + ts] * w[2:3] + b)

    for c in range(D_FF // FFN_CF):
        cv = c * FFN_CF
        cg = D_FF + cv
        up_v = jnp.dot(hn_ext, wup_ref[:, cv:cv + FFN_CF], preferred_element_type=F32)
        up_g = jnp.dot(hn_ext, wup_ref[:, cg:cg + FFN_CF], preferred_element_type=F32)
        val = conv(up_v, cv)
        gate = conv(up_g, cg)
        act_ref[:, cv:cv + FFN_CF] = (gate * jax.nn.sigmoid(gate) * val).astype(BF16)

    out = h + jnp.dot(act_ref[...], wdn_ref[...], preferred_element_type=F32)
    if final_norm:
        out = _rms(out, gfin_ref[...])
    out_ref[...] = out


def _conv_ffn(h, g, w_up, conv_w, conv_b, w_down, g_final, final_norm):
    S = h.shape[0]
    ts = FFN_TS
    hb = ts // FFN_HALO
    row = lambda i: (i, 0)
    return pl.pallas_call(
        functools.partial(_ffn_kernel, final_norm=final_norm),
        grid=(S // ts,),
        in_specs=[
            pl.BlockSpec((ts, D_MODEL), row),
            pl.BlockSpec((FFN_HALO, D_MODEL), lambda i: (jnp.maximum(i * hb - 1, 0), 0)),
            _const_spec((1, D_MODEL)),
            _const_spec(w_up.shape),
            _const_spec(conv_w.shape),
            _const_spec((1, 2 * D_FF)),
            _const_spec(w_down.shape),
            _const_spec((1, D_MODEL)),
        ],
        out_specs=pl.BlockSpec((ts, D_MODEL), row),
        out_shape=jax.ShapeDtypeStruct((S, D_MODEL), F32),
        scratch_shapes=[pltpu.VMEM((ts, D_FF), BF16)],
        compiler_params=_params("parallel"),
        name="conv_ffn",
    )(h, h, g[None, :], w_up.astype(BF16), conv_w, conv_b[None, :],
      w_down.astype(BF16), g_final[None, :])


def _s5_kernel(ulo_ref, uhi_ref, bb_ref, cc_ref, a_ref, lp_ref, w_ref, y_ref,
               up_ref, buf_ref, xb_ref, yp_ref, carry_ref):
    t = pl.program_id(1)
    nc = S5_ST // LANES
    pair = 2 * S5_ROWS

    @pl.when(t == 0)
    def _():
        carry_ref[...] = jnp.zeros(carry_ref.shape, F32)

    per_sub = S5_L // S5_ROWS
    for q in range(S5_TT // S5_ROWS):
        start = (q % per_sub) * S5_ROWS * S5_ROWS + q // per_sub
        for h, src in enumerate((ulo_ref, uhi_ref)):
            up_ref[h, pl.ds(start, S5_ROWS, stride=S5_ROWS), :] = (
                src[q * S5_ROWS:(q + 1) * S5_ROWS, :])

    up = jnp.concatenate([up_ref[0], up_ref[1]], axis=1).astype(BF16)
    buf_ref[...] = jnp.dot(up, bb_ref[0], preferred_element_type=F32)

    def col(ref_row, c):
        return jnp.broadcast_to(ref_row[:, c * LANES:(c + 1) * LANES], (S5_ROWS, LANES))

    ar = [col(a_ref[0, 0:1, :], c) for c in range(nc)]
    ai = [col(a_ref[0, 1:2, :], c) for c in range(nc)]

    def rows(l):
        return pl.ds(pl.multiple_of(l * S5_ROWS, S5_ROWS), S5_ROWS)

    def re_cols(c):
        return slice(c * LANES, (c + 1) * LANES)

    def im_cols(c):
        return slice(S5_ST + c * LANES, S5_ST + (c + 1) * LANES)

    def pass1(l, x):
        out = []
        for c in range(nc):
            xr, xi = x[2 * c], x[2 * c + 1]
            nr = ar[c] * xr - ai[c] * xi + buf_ref[rows(l), re_cols(c)]
            ni = ar[c] * xi + ai[c] * xr + buf_ref[rows(l), im_cols(c)]
            buf_ref[rows(l), re_cols(c)] = nr
            buf_ref[rows(l), im_cols(c)] = ni
            out += [nr, ni]
        return tuple(out)

    zero = jnp.zeros((S5_ROWS, LANES), F32)
    ends = lax.fori_loop(0, S5_L, pass1, (zero,) * (2 * nc), unroll=2)

    sub = lax.broadcasted_iota(jnp.int32, (S5_ROWS, LANES), 0)
    prev = []
    for c in range(nc):
        sl = slice(c * LANES, (c + 1) * LANES)
        er, ei = ends[2 * c], ends[2 * c + 1]
        d = 1
        k = 0
        while d < S5_ROWS:
            pr = lp_ref[0, k, 0:1, sl]
            pi = lp_ref[0, k, 1:2, sl]
            sr = jnp.where(sub >= d, pltpu.roll(er, d, 0), 0.0)
            si = jnp.where(sub >= d, pltpu.roll(ei, d, 0), 0.0)
            er, ei = er + pr * sr - pi * si, ei + pr * si + pi * sr
            d *= 2
            k += 1
        cr = carry_ref[0:1, sl]
        ci = carry_ref[1:2, sl]
        wr = w_ref[0, 0, :, sl]
        wi = w_ref[0, 1, :, sl]
        er, ei = er + wr * cr - wi * ci, ei + wr * ci + wi * cr
        carry_ref[0:1, sl] = er[S5_ROWS - 1:S5_ROWS, :]
        carry_ref[1:2, sl] = ei[S5_ROWS - 1:S5_ROWS, :]
        prev += [jnp.where(sub >= 1, pltpu.roll(er, 1, 0), jnp.broadcast_to(cr, (S5_ROWS, LANES))),
                 jnp.where(sub >= 1, pltpu.roll(ei, 1, 0), jnp.broadcast_to(ci, (S5_ROWS, LANES)))]

    def pass2(j, p):
        out = []
        dst = pl.ds(pl.multiple_of(j * pair, pair), pair)
        for c in range(nc):
            p0r = ar[c] * p[2 * c] - ai[c] * p[2 * c + 1]
            p0i = ar[c] * p[2 * c + 1] + ai[c] * p[2 * c]
            p1r = ar[c] * p0r - ai[c] * p0i
            p1i = ar[c] * p0i + ai[c] * p0r
            xr = jnp.concatenate([buf_ref[rows(2 * j), re_cols(c)] + p0r,
                                  buf_ref[rows(2 * j + 1), re_cols(c)] + p1r], axis=0)
            xi = jnp.concatenate([buf_ref[rows(2 * j), im_cols(c)] + p0i,
                                  buf_ref[rows(2 * j + 1), im_cols(c)] + p1i], axis=0)
            xb_ref[dst, re_cols(c)] = xr.astype(BF16)
            xb_ref[dst, im_cols(c)] = xi.astype(BF16)
            out += [p1r, p1i]
        return tuple(out)

    lax.fori_loop(0, S5_L // 2, pass2, tuple(prev))
    yp = jnp.dot(xb_ref[...], cc_ref[0], preferred_element_type=F32)
    for h in range(S5_BLK // LANES):
        yp_ref[h] = yp[:, h * LANES:(h + 1) * LANES]
    for q in range(S5_TT // S5_ROWS):
        start = (q % per_sub) * S5_ROWS * S5_ROWS + q // per_sub
        for h in range(S5_BLK // LANES):
            y_ref[q * S5_ROWS:(q + 1) * S5_ROWS, h * LANES:(h + 1) * LANES] = (
                yp_ref[h, pl.ds(start, S5_ROWS, stride=S5_ROWS), :])


def _s5_tables(lam_re, lam_im, log_dt, b_re, b_im, c_re, c_im):
    G, P, C = SSM_GROUPS, SSM_STATE, SSM_GROUP
    nb, gb = S5_NBLK, SSM_GROUPS // S5_NBLK
    dt = jnp.exp(log_dt.astype(F32))[:, None]
    lr = lam_re.astype(F32)
    li = lam_im.astype(F32)
    mag = jnp.exp(lr * dt)
    ar = mag * jnp.cos(li * dt)
    ai = mag * jnp.sin(li * dt)
    den = lr * lr + li * li
    nr = ar - 1.0
    coef_r = (nr * lr + ai * li) / den
    coef_i = (ai * lr - nr * li) / den
    br = b_re.astype(F32)
    bi = b_im.astype(F32)
    bbar_r = coef_r[..., None] * br - coef_i[..., None] * bi
    bbar_i = coef_r[..., None] * bi + coef_i[..., None] * br
    eye = jnp.eye(gb, dtype=F32)

    def blockdiag_in(m):
        return jnp.einsum('bgpc,gh->bgchp', m.reshape(nb, gb, P, C), eye).reshape(nb, gb * C, gb * P)

    def blockdiag_out(m):
        return jnp.einsum('bgcp,gh->bgphc', m.reshape(nb, gb, C, P), eye).reshape(nb, gb * P, gb * C)

    bb = jnp.concatenate([blockdiag_in(bbar_r), blockdiag_in(bbar_i)], axis=2).astype(BF16)
    cc = jnp.concatenate([blockdiag_out(c_re.astype(F32)),
                          -blockdiag_out(c_im.astype(F32))], axis=1).astype(BF16)

    def apow(nsteps):
        m = jnp.exp(lr * dt * nsteps)
        return m * jnp.cos(li * dt * nsteps), m * jnp.sin(li * dt * nsteps)

    def blk(x):
        lead = x.shape[:-2]
        x = x.reshape(lead + (nb, gb * P))
        return jnp.moveaxis(x, -2, 0)

    a_tab = jnp.stack([blk(ar), blk(ai)], axis=1)
    lp = []
    d = 1
    while d < S5_ROWS:
        pr, pi = apow(float(S5_L * d))
        lp.append(jnp.stack([blk(pr), blk(pi)], axis=1))
        d *= 2
    lp_tab = jnp.stack(lp, axis=1)
    steps = (S5_L * (jnp.arange(S5_ROWS, dtype=F32) + 1.0))[:, None, None]
    wr, wi = apow(steps)
    w_tab = jnp.stack([blk(wr), blk(wi)], axis=1)
    return bb, cc, a_tab, lp_tab, w_tab


def _s5_core(u, bb, cc, a_tab, lp_tab, w_tab):
    S = u.shape[0]
    nt = S // S5_TT
    return pl.pallas_call(
        _s5_kernel,
        grid=(S5_NBLK, nt),
        in_specs=[
            pl.BlockSpec((S5_TT, LANES), lambda b, t: (t, 2 * b)),
            pl.BlockSpec((S5_TT, LANES), lambda b, t: (t, 2 * b + 1)),
            pl.BlockSpec((1, S5_BLK, 2 * S5_ST), lambda b, t: (b, 0, 0)),
            pl.BlockSpec((1, 2 * S5_ST, S5_BLK), lambda b, t: (b, 0, 0)),
            pl.BlockSpec((1, 2, S5_ST), lambda b, t: (b, 0, 0)),
            pl.BlockSpec((1,) + lp_tab.shape[1:], lambda b, t: (b, 0, 0, 0)),
            pl.BlockSpec((1, 2, S5_ROWS, S5_ST), lambda b, t: (b, 0, 0, 0)),
        ],
        out_specs=pl.BlockSpec((S5_TT, S5_BLK), lambda b, t: (t, b)),
        out_shape=jax.ShapeDtypeStruct((S, D_MODEL), F32),
        scratch_shapes=[pltpu.VMEM((S5_BLK // LANES, S5_TT, LANES), F32),
                        pltpu.VMEM((S5_TT, 2 * S5_ST), F32),
                        pltpu.VMEM((S5_TT, 2 * S5_ST), BF16),
                        pltpu.VMEM((S5_BLK // LANES, S5_TT, LANES), F32),
                        pltpu.VMEM((2, S5_ST), F32)],
        compiler_params=_params("parallel", "arbitrary"),
        name="s5_core",
    )(u, u, bb, cc, a_tab, lp_tab, w_tab)


def _s5_glu_kernel(h_ref, y_ref, u_ref, d_ref, w_ref, out_ref):
    y = jax.nn.gelu(y_ref[...] + d_ref[...] * u_ref[...])
    z = jnp.dot(y.astype(BF16), w_ref[...], preferred_element_type=F32)
    out_ref[...] = h_ref[...] + z[:, :D_MODEL] * jax.nn.sigmoid(z[:, D_MODEL:])


def _s5_glu(h, y, u, d, w):
    S = h.shape[0]
    ts = ROW_TS
    row = lambda i: (i, 0)
    tile = pl.BlockSpec((ts, D_MODEL), row)
    return pl.pallas_call(
        _s5_glu_kernel,
        grid=(S // ts,),
        in_specs=[tile, tile, tile, _const_spec((1, D_MODEL)), _const_spec(w.shape)],
        out_specs=tile,
        out_shape=jax.ShapeDtypeStruct((S, D_MODEL), F32),
        compiler_params=_params("parallel"),
        name="s5_glu",
    )(h, y, u, d[None, :], w.astype(BF16))


def kernel(x, positions, mla_w_a, mla_g_q, mla_g_kv, mla_w_uq, mla_w_ukv, mla_w_o,
           ssm_w_in, ssm_lambda_re, ssm_lambda_im, ssm_log_dt, ssm_b_re, ssm_b_im,
           ssm_c_re, ssm_c_im, ssm_d, ssm_w_glu, ffn_w_up, ffn_conv_w, ffn_conv_b,
           ffn_w_down, g_mix, g_ffn, g_final):
    B, S, D = x.shape
    assert (B, S, D) == (1, SEQ, D_MODEL)
    h = x.reshape(S, D)
    pos = positions.reshape(S, 1)

    q, kt, v = _mla_proj(h, pos, g_mix[0], mla_w_a[0], mla_g_q[0], mla_g_kv[0],
                         mla_w_uq[0], mla_w_ukv[0])
    o = _attention(q, kt, v)
    h = _out_proj(h, o, mla_w_o[0])
    h = _conv_ffn(h, g_ffn[0], ffn_w_up[0], ffn_conv_w[0], ffn_conv_b[0], ffn_w_down[0],
                  g_final, final_norm=False)

    u = _norm_proj(h, g_mix[1], ssm_w_in[0])
    tabs = _s5_tables(ssm_lambda_re[0], ssm_lambda_im[0], ssm_log_dt[0],
                      ssm_b_re[0], ssm_b_im[0], ssm_c_re[0], ssm_c_im[0])
    y = _s5_core(u, *tabs)
    h = _s5_glu(h, y, u, ssm_d[0], ssm_w_glu[0])
    h = _conv_ffn(h, g_ffn[1], ffn_w_up[1], ffn_conv_w[1], ffn_conv_b[1], ffn_w_down[1],
                  g_final, final_norm=True)
    return h.reshape(B, S, D)
```

```python
import functools
import math

import jax
import jax.numpy as jnp
from jax import lax
from jax.experimental import pallas as pl
from jax.experimental.pallas import tpu as pltpu

F32 = jnp.float32
BF16 = jnp.bfloat16

D_MODEL = 1024
SEQ = 16384
EPS = 1e-6
CHUNK = 64
MLA_HEADS = 8
QK_NOPE = 128
QK_ROPE = 64
V_HEAD = 128
Q_LORA = 384
KV_LORA = 256
ROPE_THETA = 10000.0
SSM_GROUP = 16
SSM_GROUPS = 64
SSM_STATE = 64
D_FF = 2816
CONV_W = 3

LANES = 128
SUBLANES = 8
HEAD_PAD = 2 * LANES
VMEM_LIMIT = 56 * 1024 * 1024

PROJ_TS = 512
ATT_TQ = 512
ATT_TK = 1024
ROW_TS = 1024
FFN_TS = 512
FFN_HALO = 16
FFN_CF = 256
S5_TT = 1024
S5_ROWS = SUBLANES
S5_L = S5_TT // S5_ROWS
S5_CH = 16
S5_BLK = 2 * LANES
S5_NBLK = D_MODEL // S5_BLK
S5_ST = S5_BLK // SSM_GROUP * SSM_STATE
NEG_BIG = -1e30


def _rms(x, g):
    return x * lax.rsqrt(jnp.mean(x * x, axis=-1, keepdims=True) + EPS) * g


def _const_spec(shape):
    nd = len(shape)
    return pl.BlockSpec(shape, lambda *_: (0,) * nd, pipeline_mode=pl.Buffered(1))


def _params(*sem):
    return pltpu.CompilerParams(dimension_semantics=sem, vmem_limit_bytes=VMEM_LIMIT)


def _mla_proj_kernel(x_ref, pos_ref, gmix_ref, wa_ref, gq_ref, gkv_ref, wq_ref,
                     wkt_ref, wv_ref, inv_ref, cmask_ref, sgn_ref,
                     q_ref, kt_ref, v_ref):
    hn = _rms(x_ref[...], gmix_ref[...])
    a = jnp.dot(hn.astype(BF16), wa_ref[...], preferred_element_type=F32)
    cq = _rms(a[:, :Q_LORA], gq_ref[...])
    ckv = _rms(a[:, Q_LORA:Q_LORA + KV_LORA], gkv_ref[...])

    ang = pos_ref[...].astype(F32) * inv_ref[...]
    ca = jnp.cos(ang) * cmask_ref[...]
    sa = jnp.sin(ang) * sgn_ref[...]

    def rope(v2):
        return v2 * ca + pltpu.roll(v2, LANES // 2, 1) * sa

    scale = (QK_NOPE + QK_ROPE) ** -0.5 * math.log2(math.e)
    qraw = jnp.dot(cq.astype(BF16), wq_ref[...], preferred_element_type=F32)
    for h in range(MLA_HEADS):
        base = h * HEAD_PAD
        q_ref[h, :, 0:LANES] = (qraw[:, base:base + LANES] * scale).astype(BF16)
        q_ref[h, :, LANES:HEAD_PAD] = (
            rope(qraw[:, base + LANES:base + HEAD_PAD]) * scale).astype(BF16)

    ckv_b = ckv.astype(BF16)
    ckv_t = ckv.T.astype(BF16)
    kn_t = jnp.dot(wkt_ref[...], ckv_t, preferred_element_type=F32)
    kr_t = rope(a[:, Q_LORA + KV_LORA:]).T.astype(BF16)
    v = jnp.dot(ckv_b, wv_ref[...], preferred_element_type=F32)
    for h in range(MLA_HEADS):
        kt_ref[h, 0, 0:LANES, :] = kn_t[h * LANES:(h + 1) * LANES, :].astype(BF16)
        kt_ref[h, 0, LANES:HEAD_PAD, :] = kr_t
        v_ref[h] = v[:, h * V_HEAD:(h + 1) * V_HEAD].astype(BF16)


def _mla_proj(x, pos, g_mix, w_a, g_q, g_kv, w_uq, w_ukv):
    S = x.shape[0]
    H = MLA_HEADS
    ts = PROJ_TS
    kr = w_a[:, Q_LORA + KV_LORA:]
    kr_sw = jnp.concatenate([kr[:, QK_ROPE // 2:], kr[:, :QK_ROPE // 2]], axis=1)
    wa_ext = jnp.concatenate([w_a, kr_sw], axis=1).astype(BF16)
    wq = w_uq.reshape(Q_LORA, H, QK_NOPE + QK_ROPE)
    wq_r = wq[:, :, QK_NOPE:]
    wq_sw = jnp.concatenate([wq_r[:, :, QK_ROPE // 2:], wq_r[:, :, :QK_ROPE // 2]], axis=2)
    wq_ext = jnp.concatenate([wq, wq_sw], axis=2).reshape(Q_LORA, H * HEAD_PAD).astype(BF16)
    wkv = w_ukv.reshape(KV_LORA, H, QK_NOPE + V_HEAD)
    wk_t = wkv[:, :, :QK_NOPE].reshape(KV_LORA, H * QK_NOPE).T.astype(BF16)
    wv = wkv[:, :, QK_NOPE:].reshape(KV_LORA, H * V_HEAD).astype(BF16)
    half = QK_ROPE // 2
    inv = 1.0 / (ROPE_THETA ** (jnp.arange(0, QK_ROPE, 2, dtype=F32) / QK_ROPE))
    zeros = jnp.zeros((LANES - QK_ROPE,), F32)
    inv_row = jnp.concatenate([inv, inv, zeros])[None, :]
    cmask = jnp.concatenate([jnp.ones((QK_ROPE,), F32), zeros])[None, :]
    sgn = jnp.concatenate([-jnp.ones((half,), F32), jnp.ones((half,), F32), zeros])[None, :]

    n = S // ts
    per_kv = ATT_TK // ts
    row = lambda i: (i, 0)
    return pl.pallas_call(
        _mla_proj_kernel,
        grid=(n,),
        in_specs=[
            pl.BlockSpec((ts, D_MODEL), row),
            pl.BlockSpec((ts, 1), row),
            _const_spec((1, D_MODEL)),
            _const_spec(wa_ext.shape),
            _const_spec((1, Q_LORA)),
            _const_spec((1, KV_LORA)),
            _const_spec(wq_ext.shape),
            _const_spec(wk_t.shape),
            _const_spec(wv.shape),
            _const_spec((1, LANES)),
            _const_spec((1, LANES)),
            _const_spec((1, LANES)),
        ],
        out_specs=[
            pl.BlockSpec((H, ts, HEAD_PAD), lambda i: (0, i, 0)),
            pl.BlockSpec((H, 1, HEAD_PAD, ts), lambda i: (0, i // per_kv, 0, i % per_kv)),
            pl.BlockSpec((H, ts, V_HEAD), lambda i: (0, i, 0)),
        ],
        out_shape=[
            jax.ShapeDtypeStruct((H, S, HEAD_PAD), BF16),
            jax.ShapeDtypeStruct((H, S // ATT_TK, HEAD_PAD, ATT_TK), BF16),
            jax.ShapeDtypeStruct((H, S, V_HEAD), BF16),
        ],
        compiler_params=_params("parallel"),
        name="mla_proj",
    )(x, pos, g_mix[None, :], wa_ext, g_q[None, :], g_kv[None, :], wq_ext, wk_t, wv,
      inv_row, cmask, sgn)


def _attn_kernel(q_ref, kt_ref, v_ref, o_ref, sa_ref, sb_ref, m_ref, l_ref, acc_ref):
    i = pl.program_id(1)
    m_ref[...] = jnp.full(m_ref.shape, NEG_BIG, F32)
    l_ref[...] = jnp.zeros(l_ref.shape, F32)
    acc_ref[...] = jnp.zeros(acc_ref.shape, F32)

    def scores(j, dst):
        dst[...] = jnp.dot(q_ref[0], kt_ref[0, j], preferred_element_type=F32)

    def softmax_pv(src, j, masked):
        s = src[...]
        if masked:
            rq = (q_off + lax.broadcasted_iota(jnp.int32, s.shape, 0)) // CHUNK
            ck = lax.broadcasted_iota(jnp.int32, s.shape, 1) // CHUNK
            s = jnp.where(ck <= rq, s, NEG_BIG)
        cols = [s[:, c * LANES:(c + 1) * LANES] for c in range(ATT_TK // LANES)]
        m_prev = m_ref[...]
        m_new = jnp.maximum(m_prev, jnp.max(functools.reduce(jnp.maximum, cols),
                                            axis=1, keepdims=True))
        alpha = jnp.exp2(m_prev - m_new)
        ps = [jnp.exp2(c - m_new) for c in cols]
        l_ref[...] = alpha * l_ref[...] + functools.reduce(jnp.add, ps)
        p = jnp.concatenate(ps, axis=1)
        ks = pl.multiple_of(j * ATT_TK, ATT_TK)
        pv = jnp.dot(p.astype(BF16), v_ref[0, pl.ds(ks, ATT_TK), :],
                     preferred_element_type=F32)
        acc_ref[...] = acc_ref[...] * alpha + pv
        m_ref[...] = m_new

    kv_per_q = ATT_TK // ATT_TQ
    n_full = i // kv_per_q
    q_off = (i % kv_per_q) * ATT_TQ

    scores(0, sa_ref)

    def pair(t, carry):
        j = 2 * t
        scores(j + 1, sb_ref)
        softmax_pv(sa_ref, j, False)
        scores(j + 2, sa_ref)
        softmax_pv(sb_ref, j + 1, False)
        return carry

    lax.fori_loop(0, n_full // 2, pair, 0)

    @pl.when(n_full % 2 == 0)
    def _():
        softmax_pv(sa_ref, n_full, True)

    @pl.when(n_full % 2 == 1)
    def _():
        scores(n_full, sb_ref)
        softmax_pv(sa_ref, n_full - 1, False)
        softmax_pv(sb_ref, n_full, True)

    l = jnp.sum(l_ref[...], axis=1, keepdims=True)
    o_ref[...] = (acc_ref[...] / l).astype(o_ref.dtype)


def _attention(q, kt, v):
    H, S, _ = q.shape
    nq = S // ATT_TQ
    nk = S // ATT_TK
    return pl.pallas_call(
        _attn_kernel,
        grid=(H, nq),
        in_specs=[
            pl.BlockSpec((1, ATT_TQ, HEAD_PAD), lambda h, i: (h, i, 0)),
            pl.BlockSpec((1, nk, HEAD_PAD, ATT_TK), lambda h, i: (h, 0, 0, 0)),
            pl.BlockSpec((1, S, V_HEAD), lambda h, i: (h, 0, 0)),
        ],
        out_specs=pl.BlockSpec((ATT_TQ, V_HEAD), lambda h, i: (i, h)),
        out_shape=jax.ShapeDtypeStruct((S, H * V_HEAD), BF16),
        scratch_shapes=[
            pltpu.VMEM((ATT_TQ, ATT_TK), F32),
            pltpu.VMEM((ATT_TQ, ATT_TK), F32),
            pltpu.VMEM((ATT_TQ, LANES), F32),
            pltpu.VMEM((ATT_TQ, LANES), F32),
            pltpu.VMEM((ATT_TQ, V_HEAD), F32),
        ],
        compiler_params=_params("parallel", "arbitrary"),
        name="mla_attention",
    )(q, kt, v)


def _out_proj_kernel(h_ref, o_ref, w_ref, out_ref):
    out_ref[...] = h_ref[...] + jnp.dot(o_ref[...], w_ref[...], preferred_element_type=F32)


def _out_proj(h, o, w):
    S = h.shape[0]
    ts = ROW_TS
    row = lambda i: (i, 0)
    return pl.pallas_call(
        _out_proj_kernel,
        grid=(S // ts,),
        in_specs=[pl.BlockSpec((ts, D_MODEL), row),
                  pl.BlockSpec((ts, o.shape[1]), row),
                  _const_spec(w.shape)],
        out_specs=pl.BlockSpec((ts, D_MODEL), row),
        out_shape=jax.ShapeDtypeStruct((S, D_MODEL), F32),
        compiler_params=_params("parallel"),
        name="attn_out_proj",
    )(h, o, w.astype(BF16))


def _norm_proj_kernel(h_ref, g_ref, w_ref, out_ref):
    hn = _rms(h_ref[...], g_ref[...])
    out_ref[...] = jnp.dot(hn.astype(BF16), w_ref[...], preferred_element_type=F32)


def _norm_proj(h, g, w):
    S = h.shape[0]
    ts = ROW_TS
    row = lambda i: (i, 0)
    return pl.pallas_call(
        _norm_proj_kernel,
        grid=(S // ts,),
        in_specs=[pl.BlockSpec((ts, D_MODEL), row),
                  _const_spec((1, D_MODEL)),
                  _const_spec(w.shape)],
        out_specs=pl.BlockSpec((ts, w.shape[1]), row),
        out_shape=jax.ShapeDtypeStruct((S, w.shape[1]), F32),
        compiler_params=_params("parallel"),
        name="norm_proj",
    )(h, g[None, :], w.astype(BF16))


def _ffn_kernel(h_ref, halo_ref, g_ref, wup_ref, cw_ref, cb_ref, wdn_ref, gfin_ref,
                out_ref, act_ref, *, final_norm):
    i = pl.program_id(0)
    ts = h_ref.shape[0]
    h = h_ref[...]
    g = g_ref[...]
    hn = _rms(h, g)
    halo = _rms(halo_ref[...], g) * (i > 0).astype(F32)
    hn_ext = jnp.concatenate([halo, hn], axis=0).astype(BF16)

    def conv(up, col):
        w = cw_ref[:, col:col + FFN_CF]
        b = cb_ref[:, col:col + FFN_CF]
        return (up[FFN_HALO - 2:FFN_HALO - 2 + ts] * w[0:1]
                + up[FFN_HALO - 1:FFN_HALO - 1 + ts] * w[1:2]
                + up[FFN_HALO:FFN_HALO + ts] * w[2:3] + b)

    for c in range(D_FF // FFN_CF):
        cv = c * FFN_CF
        cg = D_FF + cv
        up_v = jnp.dot(hn_ext, wup_ref[:, cv:cv + FFN_CF], preferred_element_type=F32)
        up_g = jnp.dot(hn_ext, wup_ref[:, cg:cg + FFN_CF], preferred_element_type=F32)
        val = conv(up_v, cv)
        gate = conv(up_g, cg)
        act_ref[:, cv:cv + FFN_CF] = (gate * jax.nn.sigmoid(gate) * val).astype(BF16)

    out = h + jnp.dot(act_ref[...], wdn_ref[...], preferred_element_type=F32)
    if final_norm:
        out = _rms(out, gfin_ref[...])
    out_ref[...] = out


def _conv_ffn(h, g, w_up, conv_w, conv_b, w_down, g_final, final_norm):
    S = h.shape[0]
    ts = FFN_TS
    hb = ts // FFN_HALO
    row = lambda i: (i, 0)
    return pl.pallas_call(
        functools.partial(_ffn_kernel, final_norm=final_norm),
        grid=(S // ts,),
        in_specs=[
            pl.BlockSpec((ts, D_MODEL), row),
            pl.BlockSpec((FFN_HALO, D_MODEL), lambda i: (jnp.maximum(i * hb - 1, 0), 0)),
            _const_spec((1, D_MODEL)),
            _const_spec(w_up.shape),
            _const_spec(conv_w.shape),
            _const_spec((1, 2 * D_FF)),
            _const_spec(w_down.shape),
            _const_spec((1, D_MODEL)),
        ],
        out_specs=pl.BlockSpec((ts, D_MODEL), row),
        out_shape=jax.ShapeDtypeStruct((S, D_MODEL), F32),
        scratch_shapes=[pltpu.VMEM((ts, D_FF), BF16)],
        compiler_params=_params("parallel"),
        name="conv_ffn",
    )(h, h, g[None, :], w_up.astype(BF16), conv_w, conv_b[None, :],
      w_down.astype(BF16), g_final[None, :])


def _s5_kernel(ulo_ref, uhi_ref, bb_ref, cc_ref, a_ref, lp_ref, w_ref, y_ref,
               up_ref, buf_ref, xb_ref, yp_ref, carry_ref):
    t = pl.program_id(1)
    nc = S5_ST // LANES
    pair = 2 * S5_ROWS

    @pl.when(t == 0)
    def _():
        carry_ref[...] = jnp.zeros(carry_ref.shape, F32)

    per_sub = S5_L // S5_ROWS
    ch_rows = S5_CH * S5_ROWS
    grp_per_ch = S5_CH // S5_ROWS
    halves = S5_BLK // LANES

    def chunk_groups(k):
        for m in range(k * grp_per_ch, (k + 1) * grp_per_ch):
            for r in range(S5_ROWS):
                yield r * per_sub + m, m * S5_ROWS * S5_ROWS + r

    def col(ref_row, c):
        return jnp.broadcast_to(ref_row[:, c * LANES:(c + 1) * LANES], (S5_ROWS, LANES))

    ar = [col(a_ref[0, 0:1, :], c) for c in range(nc)]
    ai = [col(a_ref[0, 1:2, :], c) for c in range(nc)]

    def rows(l):
        return slice(l * S5_ROWS, (l + 1) * S5_ROWS)

    def re_cols(c):
        return slice(c * LANES, (c + 1) * LANES)

    def im_cols(c):
        return slice(S5_ST + c * LANES, S5_ST + (c + 1) * LANES)

    x = [jnp.zeros((S5_ROWS, LANES), F32)] * (2 * nc)
    for k in range(S5_L // S5_CH):
        for q, start in chunk_groups(k):
            for h, src in enumerate((ulo_ref, uhi_ref)):
                up_ref[h, pl.ds(start, S5_ROWS, stride=S5_ROWS), :] = (
                    src[q * S5_ROWS:(q + 1) * S5_ROWS, :])
        ck = slice(k * ch_rows, (k + 1) * ch_rows)
        up = jnp.concatenate([up_ref[h, ck, :] for h in range(halves)], axis=1).astype(BF16)
        buf_ref[ck, :] = jnp.dot(up, bb_ref[0], preferred_element_type=F32)
        for l in range(k * S5_CH, (k + 1) * S5_CH):
            for c in range(nc):
                xr, xi = x[2 * c], x[2 * c + 1]
                nr = ar[c] * xr - ai[c] * xi + buf_ref[rows(l), re_cols(c)]
                ni = ar[c] * xi + ai[c] * xr + buf_ref[rows(l), im_cols(c)]
                buf_ref[rows(l), re_cols(c)] = nr
                buf_ref[rows(l), im_cols(c)] = ni
                x[2 * c], x[2 * c + 1] = nr, ni
    ends = x

    sub = lax.broadcasted_iota(jnp.int32, (S5_ROWS, LANES), 0)
    prev = []
    for c in range(nc):
        sl = slice(c * LANES, (c + 1) * LANES)
        er, ei = ends[2 * c], ends[2 * c + 1]
        d = 1
        k = 0
        while d < S5_ROWS:
            pr = lp_ref[0, k, 0:1, sl]
            pi = lp_ref[0, k, 1:2, sl]
            sr = jnp.where(sub >= d, pltpu.roll(er, d, 0), 0.0)
            si = jnp.where(sub >= d, pltpu.roll(ei, d, 0), 0.0)
            er, ei = er + pr * sr - pi * si, ei + pr * si + pi * sr
            d *= 2
            k += 1
        cr = carry_ref[0:1, sl]
        ci = carry_ref[1:2, sl]
        wr = w_ref[0, 0, :, sl]
        wi = w_ref[0, 1, :, sl]
        er, ei = er + wr * cr - wi * ci, ei + wr * ci + wi * cr
        carry_ref[0:1, sl] = er[S5_ROWS - 1:S5_ROWS, :]
        carry_ref[1:2, sl] = ei[S5_ROWS - 1:S5_ROWS, :]
        prev += [jnp.where(sub >= 1, pltpu.roll(er, 1, 0), jnp.broadcast_to(cr, (S5_ROWS, LANES))),
                 jnp.where(sub >= 1, pltpu.roll(ei, 1, 0), jnp.broadcast_to(ci, (S5_ROWS, LANES)))]

    p = prev
    for k in range(S5_L // S5_CH):
        for j in range(k * S5_CH // 2, (k + 1) * S5_CH // 2):
            dst = slice(j * pair, (j + 1) * pair)
            for c in range(nc):
                p0r = ar[c] * p[2 * c] - ai[c] * p[2 * c + 1]
                p0i = ar[c] * p[2 * c + 1] + ai[c] * p[2 * c]
                p1r = ar[c] * p0r - ai[c] * p0i
                p1i = ar[c] * p0i + ai[c] * p0r
                xr = jnp.concatenate([buf_ref[rows(2 * j), re_cols(c)] + p0r,
                                      buf_ref[rows(2 * j + 1), re_cols(c)] + p1r], axis=0)
                xi = jnp.concatenate([buf_ref[rows(2 * j), im_cols(c)] + p0i,
                                      buf_ref[rows(2 * j + 1), im_cols(c)] + p1i], axis=0)
                xb_ref[dst, re_cols(c)] = xr.astype(BF16)
                xb_ref[dst, im_cols(c)] = xi.astype(BF16)
                p[2 * c], p[2 * c + 1] = p1r, p1i
        ck = slice(k * ch_rows, (k + 1) * ch_rows)
        yp = jnp.dot(xb_ref[ck, :], cc_ref[0], preferred_element_type=F32)
        for h in range(halves):
            yp_ref[h, ck, :] = yp[:, h * LANES:(h + 1) * LANES]
        for q, start in chunk_groups(k):
            for h in range(halves):
                y_ref[q * S5_ROWS:(q + 1) * S5_ROWS, h * LANES:(h + 1) * LANES] = (
                    yp_ref[h, pl.ds(start, S5_ROWS, stride=S5_ROWS), :])


def _s5_tables(lam_re, lam_im, log_dt, b_re, b_im, c_re, c_im):
    G, P, C = SSM_GROUPS, SSM_STATE, SSM_GROUP
    nb, gb = S5_NBLK, SSM_GROUPS // S5_NBLK
    dt = jnp.exp(log_dt.astype(F32))[:, None]
    lr = lam_re.astype(F32)
    li = lam_im.astype(F32)
    mag = jnp.exp(lr * dt)
    ar = mag * jnp.cos(li * dt)
    ai = mag * jnp.sin(li * dt)
    den = lr * lr + li * li
    nr = ar - 1.0
    coef_r = (nr * lr + ai * li) / den
    coef_i = (ai * lr - nr * li) / den
    br = b_re.astype(F32)
    bi = b_im.astype(F32)
    bbar_r = coef_r[..., None] * br - coef_i[..., None] * bi
    bbar_i = coef_r[..., None] * bi + coef_i[..., None] * br
    eye = jnp.eye(gb, dtype=F32)

    def blockdiag_in(m):
        return jnp.einsum('bgpc,gh->bgchp', m.reshape(nb, gb, P, C), eye).reshape(nb, gb * C, gb * P)

    def blockdiag_out(m):
        return jnp.einsum('bgcp,gh->bgphc', m.reshape(nb, gb, C, P), eye).reshape(nb, gb * P, gb * C)

    bb = jnp.concatenate([blockdiag_in(bbar_r), blockdiag_in(bbar_i)], axis=2).astype(BF16)
    cc = jnp.concatenate([blockdiag_out(c_re.astype(F32)),
                          -blockdiag_out(c_im.astype(F32))], axis=1).astype(BF16)

    def apow(nsteps):
        m = jnp.exp(lr * dt * nsteps)
        return m * jnp.cos(li * dt * nsteps), m * jnp.sin(li * dt * nsteps)

    def blk(x):
        lead = x.shape[:-2]
        x = x.reshape(lead + (nb, gb * P))
        return jnp.moveaxis(x, -2, 0)

    a_tab = jnp.stack([blk(ar), blk(ai)], axis=1)
    lp = []
    d = 1
    while d < S5_ROWS:
        pr, pi = apow(float(S5_L * d))
        lp.append(jnp.stack([blk(pr), blk(pi)], axis=1))
        d *= 2
    lp_tab = jnp.stack(lp, axis=1)
    steps = (S5_L * (jnp.arange(S5_ROWS, dtype=F32) + 1.0))[:, None, None]
    wr, wi = apow(steps)
    w_tab = jnp.stack([blk(wr), blk(wi)], axis=1)
    return bb, cc, a_tab, lp_tab, w_tab


def _s5_core(u, bb, cc, a_tab, lp_tab, w_tab):
    S = u.shape[0]
    nt = S // S5_TT
    return pl.pallas_call(
        _s5_kernel,
        grid=(S5_NBLK, nt),
        in_specs=[
            pl.BlockSpec((S5_TT, LANES), lambda b, t: (t, 2 * b)),
            pl.BlockSpec((S5_TT, LANES), lambda b, t: (t, 2 * b + 1)),
            pl.BlockSpec((1, S5_BLK, 2 * S5_ST), lambda b, t: (b, 0, 0)),
            pl.BlockSpec((1, 2 * S5_ST, S5_BLK), lambda b, t: (b, 0, 0)),
            pl.BlockSpec((1, 2, S5_ST), lambda b, t: (b, 0, 0)),
            pl.BlockSpec((1,) + lp_tab.shape[1:], lambda b, t: (b, 0, 0, 0)),
            pl.BlockSpec((1, 2, S5_ROWS, S5_ST), lambda b, t: (b, 0, 0, 0)),
        ],
        out_specs=pl.BlockSpec((S5_TT, S5_BLK), lambda b, t: (t, b)),
        out_shape=jax.ShapeDtypeStruct((S, D_MODEL), F32),
        scratch_shapes=[pltpu.VMEM((S5_BLK // LANES, S5_TT, LANES), F32),
                        pltpu.VMEM((S5_TT, 2 * S5_ST), F32),
                        pltpu.VMEM((S5_TT, 2 * S5_ST), BF16),
                        pltpu.VMEM((S5_BLK // LANES, S5_TT, LANES), F32),
                        pltpu.VMEM((2, S5_ST), F32)],
        compiler_params=_params("parallel", "arbitrary"),
        name="s5_core",
    )(u, u, bb, cc, a_tab, lp_tab, w_tab)


def _s5_glu_kernel(h_ref, y_ref, u_ref, d_ref, w_ref, out_ref):
    y = jax.nn.gelu(y_ref[...] + d_ref[...] * u_ref[...])
    z = jnp.dot(y.astype(BF16), w_ref[...], preferred_element_type=F32)
    out_ref[...] = h_ref[...] + z[:, :D_MODEL] * jax.nn.sigmoid(z[:, D_MODEL:])


def _s5_glu(h, y, u, d, w):
    S = h.shape[0]
    ts = ROW_TS
    row = lambda i: (i, 0)
    tile = pl.BlockSpec((ts, D_MODEL), row)
    return pl.pallas_call(
        _s5_glu_kernel,
        grid=(S // ts,),
        in_specs=[tile, tile, tile, _const_spec((1, D_MODEL)), _const_spec(w.shape)],
        out_specs=tile,
        out_shape=jax.ShapeDtypeStruct((S, D_MODEL), F32),
        compiler_params=_params("parallel"),
        name="s5_glu",
    )(h, y, u, d[None, :], w.astype(BF16))


def kernel(x, positions, mla_w_a, mla_g_q, mla_g_kv, mla_w_uq, mla_w_ukv, mla_w_o,
           ssm_w_in, ssm_lambda_re, ssm_lambda_im, ssm_log_dt, ssm_b_re, ssm_b_im,
           ssm_c_re, ssm_c_im, ssm_d, ssm_w_glu, ffn_w_up, ffn_conv_w, ffn_conv_b,
           ffn_w_down, g_mix, g_ffn, g_final):
    B, S, D = x.shape
    assert (B, S, D) == (1, SEQ, D_MODEL)
    h = x.reshape(S, D)
    pos = positions.reshape(S, 1)

    q, kt, v = _mla_proj(h, pos, g_mix[0], mla_w_a[0], mla_g_q[0], mla_g_kv[0],
                         mla_w_uq[0], mla_w_ukv[0])
    o = _attention(q, kt, v)
    h = _out_proj(h, o, mla_w_o[0])
    h = _conv_ffn(h, g_ffn[0], ffn_w_up[0], ffn_conv_w[0], ffn_conv_b[0], ffn_w_down[0],
                  g_final, final_norm=False)

    u = _norm_proj(h, g_mix[1], ssm_w_in[0])
    tabs = _s5_tables(ssm_lambda_re[0], ssm_lambda_im[0], ssm_log_dt[0],
                      ssm_b_re[0], ssm_b_im[0], ssm_c_re[0], ssm_c_im[0])
    y = _s5_core(u, *tabs)
    h = _s5_glu(h, y, u, ssm_d[0], ssm_w_glu[0])
    h = _conv_ffn(h, g_ffn[1], ffn_w_up[1], ffn_conv_w[1], ffn_conv_b[1], ffn_w_down[1],
                  g_final, final_norm=True)
    return h.reshape(B, S, D)
```

```python
import functools
import math

import jax
import jax.numpy as jnp
from jax import lax
from jax.experimental import pallas as pl
from jax.experimental.pallas import tpu as pltpu

F32 = jnp.float32
BF16 = jnp.bfloat16

D_MODEL = 1024
SEQ = 16384
EPS = 1e-6
CHUNK = 64
MLA_HEADS = 8
QK_NOPE = 128
QK_ROPE = 64
V_HEAD = 128
Q_LORA = 384
KV_LORA = 256
ROPE_THETA = 10000.0
SSM_GROUP = 16
SSM_GROUPS = 64
SSM_STATE = 64
D_FF = 2816
CONV_W = 3

LANES = 128
SUBLANES = 8
HEAD_PAD = 2 * LANES
VMEM_LIMIT = 56 * 1024 * 1024

PROJ_TS = 512
ATT_TQ = 1024
ATT_TK = 1024
FFN_TS = 512
FFN_TAIL = SUBLANES
FFN_CF = 256
S5_TT = 1024
S5_ROWS = SUBLANES
S5_L = S5_TT // S5_ROWS
S5_CH = 16
S5_BLK = 2 * LANES
S5_NBLK = D_MODEL // S5_BLK
S5_ST = S5_BLK // SSM_GROUP * SSM_STATE
NEG_BIG = -1e30


def _rms(x, g):
    return x * lax.rsqrt(jnp.mean(x * x, axis=-1, keepdims=True) + EPS) * g


def _const_spec(shape):
    nd = len(shape)
    return pl.BlockSpec(shape, lambda *_: (0,) * nd, pipeline_mode=pl.Buffered(1))


def _params(*sem):
    return pltpu.CompilerParams(dimension_semantics=sem, vmem_limit_bytes=VMEM_LIMIT)


def _mla_proj_kernel(x_ref, pos_ref, gmix_ref, wa_ref, gq_ref, gkv_ref, wq_ref,
                     wkt_ref, wv_ref, inv_ref, cmask_ref, sgn_ref,
                     q_ref, kt_ref, v_ref):
    hn = _rms(x_ref[...], gmix_ref[...])
    a = jnp.dot(hn.astype(BF16), wa_ref[...], preferred_element_type=F32)
    cq = _rms(a[:, :Q_LORA], gq_ref[...])
    ckv = _rms(a[:, Q_LORA:Q_LORA + KV_LORA], gkv_ref[...])

    ang = pos_ref[...].astype(F32) * inv_ref[...]
    ca = jnp.cos(ang) * cmask_ref[...]
    sa = jnp.sin(ang) * sgn_ref[...]

    def rope(v2):
        return v2 * ca + pltpu.roll(v2, LANES // 2, 1) * sa

    scale = (QK_NOPE + QK_ROPE) ** -0.5 * math.log2(math.e)
    qraw = jnp.dot(cq.astype(BF16), wq_ref[...], preferred_element_type=F32)
    for h in range(MLA_HEADS):
        base = h * HEAD_PAD
        q_ref[h, :, 0:LANES] = (qraw[:, base:base + LANES] * scale).astype(BF16)
        q_ref[h, :, LANES:HEAD_PAD] = (
            rope(qraw[:, base + LANES:base + HEAD_PAD]) * scale).astype(BF16)

    ckv_b = ckv.astype(BF16)
    ckv_t = ckv.T.astype(BF16)
    kn_t = jnp.dot(wkt_ref[...], ckv_t, preferred_element_type=F32)
    kr_t = rope(a[:, Q_LORA + KV_LORA:]).T.astype(BF16)
    v = jnp.dot(ckv_b, wv_ref[...], preferred_element_type=F32)
    for h in range(MLA_HEADS):
        kt_ref[h, 0, 0:LANES, :] = kn_t[h * LANES:(h + 1) * LANES, :].astype(BF16)
        kt_ref[h, 0, LANES:HEAD_PAD, :] = kr_t
        v_ref[h] = v[:, h * V_HEAD:(h + 1) * V_HEAD].astype(BF16)


def _mla_proj(x, pos, g_mix, w_a, g_q, g_kv, w_uq, w_ukv):
    S = x.shape[0]
    H = MLA_HEADS
    ts = PROJ_TS
    kr = w_a[:, Q_LORA + KV_LORA:]
    kr_sw = jnp.concatenate([kr[:, QK_ROPE // 2:], kr[:, :QK_ROPE // 2]], axis=1)
    wa_ext = jnp.concatenate([w_a, kr_sw], axis=1).astype(BF16)
    wq = w_uq.reshape(Q_LORA, H, QK_NOPE + QK_ROPE)
    wq_r = wq[:, :, QK_NOPE:]
    wq_sw = jnp.concatenate([wq_r[:, :, QK_ROPE // 2:], wq_r[:, :, :QK_ROPE // 2]], axis=2)
    wq_ext = jnp.concatenate([wq, wq_sw], axis=2).reshape(Q_LORA, H * HEAD_PAD).astype(BF16)
    wkv = w_ukv.reshape(KV_LORA, H, QK_NOPE + V_HEAD)
    wk_t = wkv[:, :, :QK_NOPE].reshape(KV_LORA, H * QK_NOPE).T.astype(BF16)
    wv = wkv[:, :, QK_NOPE:].reshape(KV_LORA, H * V_HEAD).astype(BF16)
    half = QK_ROPE // 2
    inv = 1.0 / (ROPE_THETA ** (jnp.arange(0, QK_ROPE, 2, dtype=F32) / QK_ROPE))
    zeros = jnp.zeros((LANES - QK_ROPE,), F32)
    inv_row = jnp.concatenate([inv, inv, zeros])[None, :]
    cmask = jnp.concatenate([jnp.ones((QK_ROPE,), F32), zeros])[None, :]
    sgn = jnp.concatenate([-jnp.ones((half,), F32), jnp.ones((half,), F32), zeros])[None, :]

    n = S // ts
    per_kv = ATT_TK // ts
    row = lambda i: (i, 0)
    return pl.pallas_call(
        _mla_proj_kernel,
        grid=(n,),
        in_specs=[
            pl.BlockSpec((ts, D_MODEL), row),
            pl.BlockSpec((ts, 1), row),
            _const_spec((1, D_MODEL)),
            _const_spec(wa_ext.shape),
            _const_spec((1, Q_LORA)),
            _const_spec((1, KV_LORA)),
            _const_spec(wq_ext.shape),
            _const_spec(wk_t.shape),
            _const_spec(wv.shape),
            _const_spec((1, LANES)),
            _const_spec((1, LANES)),
            _const_spec((1, LANES)),
        ],
        out_specs=[
            pl.BlockSpec((H, ts, HEAD_PAD), lambda i: (0, i, 0)),
            pl.BlockSpec((H, 1, HEAD_PAD, ts), lambda i: (0, i // per_kv, 0, i % per_kv)),
            pl.BlockSpec((H, ts, V_HEAD), lambda i: (0, i, 0)),
        ],
        out_shape=[
            jax.ShapeDtypeStruct((H, S, HEAD_PAD), BF16),
            jax.ShapeDtypeStruct((H, S // ATT_TK, HEAD_PAD, ATT_TK), BF16),
            jax.ShapeDtypeStruct((H, S, V_HEAD), BF16),
        ],
        compiler_params=_params("parallel"),
        name="mla_proj",
    )(x, pos, g_mix[None, :], wa_ext, g_q[None, :], g_kv[None, :], wq_ext, wk_t, wv,
      inv_row, cmask, sgn)


def _attn_kernel(q_ref, kt_ref, v_ref, o_ref, sa_ref, sb_ref, m_ref, l_ref, acc_ref):
    i = pl.program_id(1)
    m_ref[...] = jnp.full(m_ref.shape, NEG_BIG, F32)
    l_ref[...] = jnp.zeros(l_ref.shape, F32)
    acc_ref[...] = jnp.zeros(acc_ref.shape, F32)

    def scores(j, dst):
        dst[...] = jnp.dot(q_ref[0], kt_ref[0, j], preferred_element_type=F32)

    def softmax_pv(src, j, masked):
        s = src[...]
        if masked:
            rq = (q_off + lax.broadcasted_iota(jnp.int32, s.shape, 0)) // CHUNK
            ck = lax.broadcasted_iota(jnp.int32, s.shape, 1) // CHUNK
            s = jnp.where(ck <= rq, s, NEG_BIG)
        cols = [s[:, c * LANES:(c + 1) * LANES] for c in range(ATT_TK // LANES)]
        m_prev = m_ref[...]
        m_new = jnp.maximum(m_prev, jnp.max(functools.reduce(jnp.maximum, cols),
                                            axis=1, keepdims=True))
        alpha = jnp.exp2(m_prev - m_new)
        ps = [jnp.exp2(c - m_new) for c in cols]
        l_ref[...] = alpha * l_ref[...] + functools.reduce(jnp.add, ps)
        p = jnp.concatenate(ps, axis=1)
        ks = pl.multiple_of(j * ATT_TK, ATT_TK)
        pv = jnp.dot(p.astype(BF16), v_ref[0, pl.ds(ks, ATT_TK), :],
                     preferred_element_type=F32)
        acc_ref[...] = acc_ref[...] * alpha + pv
        m_ref[...] = m_new

    kv_per_q = ATT_TK // ATT_TQ
    n_full = i // kv_per_q
    q_off = (i % kv_per_q) * ATT_TQ

    scores(0, sa_ref)

    def pair(t, carry):
        j = 2 * t
        scores(j + 1, sb_ref)
        softmax_pv(sa_ref, j, False)
        scores(j + 2, sa_ref)
        softmax_pv(sb_ref, j + 1, False)
        return carry

    lax.fori_loop(0, n_full // 2, pair, 0)

    @pl.when(n_full % 2 == 0)
    def _():
        softmax_pv(sa_ref, n_full, True)

    @pl.when(n_full % 2 == 1)
    def _():
        scores(n_full, sb_ref)
        softmax_pv(sa_ref, n_full - 1, False)
        softmax_pv(sb_ref, n_full, True)

    l = jnp.sum(l_ref[...], axis=1, keepdims=True)
    o_ref[...] = (acc_ref[...] / l).astype(o_ref.dtype)


def _attention(q, kt, v):
    H, S, _ = q.shape
    nq = S // ATT_TQ
    nk = S // ATT_TK
    return pl.pallas_call(
        _attn_kernel,
        grid=(H, nq),
        in_specs=[
            pl.BlockSpec((1, ATT_TQ, HEAD_PAD), lambda h, i: (h, i, 0)),
            pl.BlockSpec((1, nk, HEAD_PAD, ATT_TK), lambda h, i: (h, 0, 0, 0)),
            pl.BlockSpec((1, S, V_HEAD), lambda h, i: (h, 0, 0)),
        ],
        out_specs=pl.BlockSpec((ATT_TQ, V_HEAD), lambda h, i: (i, h)),
        out_shape=jax.ShapeDtypeStruct((S, H * V_HEAD), BF16),
        scratch_shapes=[
            pltpu.VMEM((ATT_TQ, ATT_TK), F32),
            pltpu.VMEM((ATT_TQ, ATT_TK), F32),
            pltpu.VMEM((ATT_TQ, LANES), F32),
            pltpu.VMEM((ATT_TQ, LANES), F32),
            pltpu.VMEM((ATT_TQ, V_HEAD), F32),
        ],
        compiler_params=_params("parallel", "arbitrary"),
        name="mla_attention",
    )(q, kt, v)


def _ffn_body(h, i, g_ref, wup_ref, cw_ref, cb_ref, wdn_ref, act_ref, tail_ref):
    ts = h.shape[0]
    hn = _rms(h, g_ref[...]).astype(BF16)

    @pl.when(i == 0)
    def _():
        tail_ref[...] = jnp.zeros(tail_ref.shape, F32)

    def conv_up(col):
        cols = slice(col, col + FFN_CF)
        up = jnp.dot(hn, wup_ref[:, cols], preferred_element_type=F32)
        ext = jnp.concatenate([tail_ref[:, cols], up], axis=0)
        tail_ref[:, cols] = up[ts - FFN_TAIL:, :]
        w = cw_ref[:, cols]
        return (ext[FFN_TAIL - 2:FFN_TAIL - 2 + ts] * w[0:1]
                + ext[FFN_TAIL - 1:FFN_TAIL - 1 + ts] * w[1:2]
                + up * w[2:3] + cb_ref[:, cols])

    for c in range(D_FF // FFN_CF):
        val = conv_up(c * FFN_CF)
        gate = conv_up(D_FF + c * FFN_CF)
        act_ref[:, c * FFN_CF:(c + 1) * FFN_CF] = (gate * jax.nn.sigmoid(gate) * val).astype(BF16)

    return h + jnp.dot(act_ref[...], wdn_ref[...], preferred_element_type=F32)


def _ffn0_kernel(x_ref, o_ref, wo_ref, g_ref, wup_ref, cw_ref, cb_ref, wdn_ref, gmix_ref,
                 win_ref, h_out_ref, u_out_ref, act_ref, tail_ref):
    i = pl.program_id(0)
    h1 = x_ref[...] + jnp.dot(o_ref[...], wo_ref[...], preferred_element_type=F32)
    h2 = _ffn_body(h1, i, g_ref, wup_ref, cw_ref, cb_ref, wdn_ref, act_ref, tail_ref)
    h_out_ref[...] = h2
    u_out_ref[...] = jnp.dot(_rms(h2, gmix_ref[...]).astype(BF16), win_ref[...],
                             preferred_element_type=F32)


def _ffn1_kernel(h_ref, y_ref, u_ref, d_ref, wglu_ref, g_ref, wup_ref, cw_ref, cb_ref,
                 wdn_ref, gfin_ref, out_ref, act_ref, tail_ref):
    i = pl.program_id(0)
    yy = jax.nn.gelu(y_ref[...] + d_ref[...] * u_ref[...])
    z = jnp.dot(yy.astype(BF16), wglu_ref[...], preferred_element_type=F32)
    h3 = h_ref[...] + z[:, :D_MODEL] * jax.nn.sigmoid(z[:, D_MODEL:])
    out = _ffn_body(h3, i, g_ref, wup_ref, cw_ref, cb_ref, wdn_ref, act_ref, tail_ref)
    out_ref[...] = _rms(out, gfin_ref[...])


def _ffn_specs(w_up, conv_w, w_down):
    return [_const_spec((1, D_MODEL)), _const_spec(w_up.shape), _const_spec(conv_w.shape),
            _const_spec((1, 2 * D_FF)), _const_spec(w_down.shape)]


def _ffn_scratch():
    return [pltpu.VMEM((FFN_TS, D_FF), BF16), pltpu.VMEM((FFN_TAIL, 2 * D_FF), F32)]


def _attn_out_ffn_s5_in(x, o, w_o, g_ffn, w_up, conv_w, conv_b, w_down, g_mix, w_in):
    S = x.shape[0]
    row = lambda i: (i, 0)
    tile = pl.BlockSpec((FFN_TS, D_MODEL), row)
    return pl.pallas_call(
        _ffn0_kernel,
        grid=(S // FFN_TS,),
        in_specs=[tile, tile, _const_spec(w_o.shape)] + _ffn_specs(w_up, conv_w, w_down)
                 + [_const_spec((1, D_MODEL)), _const_spec(w_in.shape)],
        out_specs=[tile, tile],
        out_shape=[jax.ShapeDtypeStruct((S, D_MODEL), F32)] * 2,
        scratch_shapes=_ffn_scratch(),
        compiler_params=_params("arbitrary"),
        name="conv_ffn0",
    )(x, o, w_o.astype(BF16), g_ffn[None, :], w_up.astype(BF16), conv_w, conv_b[None, :],
      w_down.astype(BF16), g_mix[None, :], w_in.astype(BF16))


def _s5_glu_ffn_final(h, y, u, d, w_glu, g_ffn, w_up, conv_w, conv_b, w_down, g_final):
    S = h.shape[0]
    row = lambda i: (i, 0)
    tile = pl.BlockSpec((FFN_TS, D_MODEL), row)
    return pl.pallas_call(
        _ffn1_kernel,
        grid=(S // FFN_TS,),
        in_specs=[tile, tile, tile, _const_spec((1, D_MODEL)), _const_spec(w_glu.shape)]
                 + _ffn_specs(w_up, conv_w, w_down) + [_const_spec((1, D_MODEL))],
        out_specs=tile,
        out_shape=jax.ShapeDtypeStruct((S, D_MODEL), F32),
        scratch_shapes=_ffn_scratch(),
        compiler_params=_params("arbitrary"),
        name="conv_ffn1",
    )(h, y, u, d[None, :], w_glu.astype(BF16), g_ffn[None, :], w_up.astype(BF16), conv_w,
      conv_b[None, :], w_down.astype(BF16), g_final[None, :])


def _s5_kernel(ulo_ref, uhi_ref, bb_ref, cc_ref, a_ref, lp_ref, w_ref, y_ref,
               up_ref, buf_ref, xb_ref, yp_ref, carry_ref):
    t = pl.program_id(1)
    nc = S5_ST // LANES
    pair = 2 * S5_ROWS

    @pl.when(t == 0)
    def _():
        carry_ref[...] = jnp.zeros(carry_ref.shape, F32)

    per_sub = S5_L // S5_ROWS
    ch_rows = S5_CH * S5_ROWS
    grp_per_ch = S5_CH // S5_ROWS
    halves = S5_BLK // LANES

    def chunk_groups(k):
        for m in range(k * grp_per_ch, (k + 1) * grp_per_ch):
            for r in range(S5_ROWS):
                yield r * per_sub + m, m * S5_ROWS * S5_ROWS + r

    def col(ref_row, c):
        return jnp.broadcast_to(ref_row[:, c * LANES:(c + 1) * LANES], (S5_ROWS, LANES))

    ar = [col(a_ref[0, 0:1, :], c) for c in range(nc)]
    ai = [col(a_ref[0, 1:2, :], c) for c in range(nc)]

    def rows(l):
        return slice(l * S5_ROWS, (l + 1) * S5_ROWS)

    def re_cols(c):
        return slice(c * LANES, (c + 1) * LANES)

    def im_cols(c):
        return slice(S5_ST + c * LANES, S5_ST + (c + 1) * LANES)

    x = [jnp.zeros((S5_ROWS, LANES), F32)] * (2 * nc)
    for k in range(S5_L // S5_CH):
        for q, start in chunk_groups(k):
            for h, src in enumerate((ulo_ref, uhi_ref)):
                up_ref[h, pl.ds(start, S5_ROWS, stride=S5_ROWS), :] = (
                    src[q * S5_ROWS:(q + 1) * S5_ROWS, :])
        ck = slice(k * ch_rows, (k + 1) * ch_rows)
        up = jnp.concatenate([up_ref[h, ck, :] for h in range(halves)], axis=1).astype(BF16)
        buf_ref[ck, :] = jnp.dot(up, bb_ref[0], preferred_element_type=F32)
        for l in range(k * S5_CH, (k + 1) * S5_CH):
            for c in range(nc):
                xr, xi = x[2 * c], x[2 * c + 1]
                nr = ar[c] * xr - ai[c] * xi + buf_ref[rows(l), re_cols(c)]
                ni = ar[c] * xi + ai[c] * xr + buf_ref[rows(l), im_cols(c)]
                buf_ref[rows(l), re_cols(c)] = nr
                buf_ref[rows(l), im_cols(c)] = ni
                x[2 * c], x[2 * c + 1] = nr, ni
    ends = x

    sub = lax.broadcasted_iota(jnp.int32, (S5_ROWS, LANES), 0)
    prev = []
    for c in range(nc):
        sl = slice(c * LANES, (c + 1) * LANES)
        er, ei = ends[2 * c], ends[2 * c + 1]
        d = 1
        k = 0
        while d < S5_ROWS:
            pr = lp_ref[0, k, 0:1, sl]
            pi = lp_ref[0, k, 1:2, sl]
            sr = jnp.where(sub >= d, pltpu.roll(er, d, 0), 0.0)
            si = jnp.where(sub >= d, pltpu.roll(ei, d, 0), 0.0)
            er, ei = er + pr * sr - pi * si, ei + pr * si + pi * sr
            d *= 2
            k += 1
        cr = carry_ref[0:1, sl]
        ci = carry_ref[1:2, sl]
        wr = w_ref[0, 0, :, sl]
        wi = w_ref[0, 1, :, sl]
        er, ei = er + wr * cr - wi * ci, ei + wr * ci + wi * cr
        carry_ref[0:1, sl] = er[S5_ROWS - 1:S5_ROWS, :]
        carry_ref[1:2, sl] = ei[S5_ROWS - 1:S5_ROWS, :]
        prev += [jnp.where(sub >= 1, pltpu.roll(er, 1, 0), jnp.broadcast_to(cr, (S5_ROWS, LANES))),
                 jnp.where(sub >= 1, pltpu.roll(ei, 1, 0), jnp.broadcast_to(ci, (S5_ROWS, LANES)))]

    p = prev
    for k in range(S5_L // S5_CH):
        for j in range(k * S5_CH // 2, (k + 1) * S5_CH // 2):
            dst = slice(j * pair, (j + 1) * pair)
            for c in range(nc):
                p0r = ar[c] * p[2 * c] - ai[c] * p[2 * c + 1]
                p0i = ar[c] * p[2 * c + 1] + ai[c] * p[2 * c]
                p1r = ar[c] * p0r - ai[c] * p0i
                p1i = ar[c] * p0i + ai[c] * p0r
                xr = jnp.concatenate([buf_ref[rows(2 * j), re_cols(c)] + p0r,
                                      buf_ref[rows(2 * j + 1), re_cols(c)] + p1r], axis=0)
                xi = jnp.concatenate([buf_ref[rows(2 * j), im_cols(c)] + p0i,
                                      buf_ref[rows(2 * j + 1), im_cols(c)] + p1i], axis=0)
                xb_ref[dst, re_cols(c)] = xr.astype(BF16)
                xb_ref[dst, im_cols(c)] = xi.astype(BF16)
                p[2 * c], p[2 * c + 1] = p1r, p1i
        ck = slice(k * ch_rows, (k + 1) * ch_rows)
        yp = jnp.dot(xb_ref[ck, :], cc_ref[0], preferred_element_type=F32)
        for h in range(halves):
            yp_ref[h, ck, :] = yp[:, h * LANES:(h + 1) * LANES]
        for q, start in chunk_groups(k):
            for h in range(halves):
                y_ref[q * S5_ROWS:(q + 1) * S5_ROWS, h * LANES:(h + 1) * LANES] = (
                    yp_ref[h, pl.ds(start, S5_ROWS, stride=S5_ROWS), :])


def _s5_tables(lam_re, lam_im, log_dt, b_re, b_im, c_re, c_im):
    G, P, C = SSM_GROUPS, SSM_STATE, SSM_GROUP
    nb, gb = S5_NBLK, SSM_GROUPS // S5_NBLK
    dt = jnp.exp(log_dt.astype(F32))[:, None]
    lr = lam_re.astype(F32)
    li = lam_im.astype(F32)
    mag = jnp.exp(lr * dt)
    ar = mag * jnp.cos(li * dt)
    ai = mag * jnp.sin(li * dt)
    den = lr * lr + li * li
    nr = ar - 1.0
    coef_r = (nr * lr + ai * li) / den
    coef_i = (ai * lr - nr * li) / den
    br = b_re.astype(F32)
    bi = b_im.astype(F32)
    bbar_r = coef_r[..., None] * br - coef_i[..., None] * bi
    bbar_i = coef_r[..., None] * bi + coef_i[..., None] * br
    eye = jnp.eye(gb, dtype=F32)

    def blockdiag_in(m):
        return jnp.einsum('bgpc,gh->bgchp', m.reshape(nb, gb, P, C), eye).reshape(nb, gb * C, gb * P)

    def blockdiag_out(m):
        return jnp.einsum('bgcp,gh->bgphc', m.reshape(nb, gb, C, P), eye).reshape(nb, gb * P, gb * C)

    bb = jnp.concatenate([blockdiag_in(bbar_r), blockdiag_in(bbar_i)], axis=2).astype(BF16)
    cc = jnp.concatenate([blockdiag_out(c_re.astype(F32)),
                          -blockdiag_out(c_im.astype(F32))], axis=1).astype(BF16)

    def apow(nsteps):
        m = jnp.exp(lr * dt * nsteps)
        return m * jnp.cos(li * dt * nsteps), m * jnp.sin(li * dt * nsteps)

    def blk(x):
        lead = x.shape[:-2]
        x = x.reshape(lead + (nb, gb * P))
        return jnp.moveaxis(x, -2, 0)

    a_tab = jnp.stack([blk(ar), blk(ai)], axis=1)
    lp = []
    d = 1
    while d < S5_ROWS:
        pr, pi = apow(float(S5_L * d))
        lp.append(jnp.stack([blk(pr), blk(pi)], axis=1))
        d *= 2
    lp_tab = jnp.stack(lp, axis=1)
    steps = (S5_L * (jnp.arange(S5_ROWS, dtype=F32) + 1.0))[:, None, None]
    wr, wi = apow(steps)
    w_tab = jnp.stack([blk(wr), blk(wi)], axis=1)
    return bb, cc, a_tab, lp_tab, w_tab


def _s5_core(u, bb, cc, a_tab, lp_tab, w_tab):
    S = u.shape[0]
    nt = S // S5_TT
    return pl.pallas_call(
        _s5_kernel,
        grid=(S5_NBLK, nt),
        in_specs=[
            pl.BlockSpec((S5_TT, LANES), lambda b, t: (t, 2 * b)),
            pl.BlockSpec((S5_TT, LANES), lambda b, t: (t, 2 * b + 1)),
            pl.BlockSpec((1, S5_BLK, 2 * S5_ST), lambda b, t: (b, 0, 0)),
            pl.BlockSpec((1, 2 * S5_ST, S5_BLK), lambda b, t: (b, 0, 0)),
            pl.BlockSpec((1, 2, S5_ST), lambda b, t: (b, 0, 0)),
            pl.BlockSpec((1,) + lp_tab.shape[1:], lambda b, t: (b, 0, 0, 0)),
            pl.BlockSpec((1, 2, S5_ROWS, S5_ST), lambda b, t: (b, 0, 0, 0)),
        ],
        out_specs=pl.BlockSpec((S5_TT, S5_BLK), lambda b, t: (t, b)),
        out_shape=jax.ShapeDtypeStruct((S, D_MODEL), F32),
        scratch_shapes=[pltpu.VMEM((S5_BLK // LANES, S5_TT, LANES), F32),
                        pltpu.VMEM((S5_TT, 2 * S5_ST), F32),
                        pltpu.VMEM((S5_TT, 2 * S5_ST), BF16),
                        pltpu.VMEM((S5_BLK // LANES, S5_TT, LANES), F32),
                        pltpu.VMEM((2, S5_ST), F32)],
        compiler_params=_params("parallel", "arbitrary"),
        name="s5_core",
    )(u, u, bb, cc, a_tab, lp_tab, w_tab)


def kernel(x, positions, mla_w_a, mla_g_q, mla_g_kv, mla_w_uq, mla_w_ukv, mla_w_o,
           ssm_w_in, ssm_lambda_re, ssm_lambda_im, ssm_log_dt, ssm_b_re, ssm_b_im,
           ssm_c_re, ssm_c_im, ssm_d, ssm_w_glu, ffn_w_up, ffn_conv_w, ffn_conv_b,
           ffn_w_down, g_mix, g_ffn, g_final):
    B, S, D = x.shape
    assert (B, S, D) == (1, SEQ, D_MODEL)
    h = x.reshape(S, D)
    pos = positions.reshape(S, 1)

    q, kt, v = _mla_proj(h, pos, g_mix[0], mla_w_a[0], mla_g_q[0], mla_g_kv[0],
                         mla_w_uq[0], mla_w_ukv[0])
    o = _attention(q, kt, v)
    h, u = _attn_out_ffn_s5_in(h, o, mla_w_o[0], g_ffn[0], ffn_w_up[0], ffn_conv_w[0],
                               ffn_conv_b[0], ffn_w_down[0], g_mix[1], ssm_w_in[0])

    tabs = _s5_tables(ssm_lambda_re[0], ssm_lambda_im[0], ssm_log_dt[0],
                      ssm_b_re[0], ssm_b_im[0], ssm_c_re[0], ssm_c_im[0])
    y = _s5_core(u, *tabs)
    h = _s5_glu_ffn_final(h, y, u, ssm_d[0], ssm_w_glu[0], g_ffn[1], ffn_w_up[1],
                          ffn_conv_w[1], ffn_conv_b[1], ffn_w_down[1], g_final)
    return h.reshape(B, S, D)
```

```python
import functools
import math

import jax
import jax.numpy as jnp
from jax import lax
from jax.experimental import pallas as pl
from jax.experimental.pallas import tpu as pltpu

F32 = jnp.float32
BF16 = jnp.bfloat16

D_MODEL = 1024
SEQ = 16384
EPS = 1e-6
CHUNK = 64
MLA_HEADS = 8
QK_NOPE = 128
QK_ROPE = 64
V_HEAD = 128
Q_LORA = 384
KV_LORA = 256
ROPE_THETA = 10000.0
SSM_GROUP = 16
SSM_GROUPS = 64
SSM_STATE = 64
D_FF = 2816
CONV_W = 3

LANES = 128
SUBLANES = 8
HEAD_PAD = 2 * LANES
VMEM_LIMIT = 56 * 1024 * 1024

PROJ_TS = 512
ATT_TQ = 1024
ATT_TK = 1024
FFN_TS = 512
FFN_TAIL = SUBLANES
FFN_CF = 256
S5_TT = 1024
S5_ROWS = SUBLANES
S5_L = S5_TT // S5_ROWS
S5_CH = 16
S5_BLK = 2 * LANES
S5_NBLK = D_MODEL // S5_BLK
S5_ST = S5_BLK // SSM_GROUP * SSM_STATE
NEG_BIG = -1e30


def _rms(x, g):
    return x * lax.rsqrt(jnp.mean(x * x, axis=-1, keepdims=True) + EPS) * g


def _const_spec(shape):
    nd = len(shape)
    return pl.BlockSpec(shape, lambda *_: (0,) * nd, pipeline_mode=pl.Buffered(1))


def _params(*sem):
    return pltpu.CompilerParams(dimension_semantics=sem, vmem_limit_bytes=VMEM_LIMIT)


def _mla_proj_kernel(x_ref, pos_ref, gmix_ref, wa_ref, gq_ref, gkv_ref, wqt_ref,
                     wk_ref, wvt_ref, inv_ref, qt_ref, k_ref, vt_ref):
    hn = _rms(x_ref[...], gmix_ref[...])
    a = jnp.dot(hn.astype(BF16), wa_ref[...], preferred_element_type=F32)
    cq = _rms(a[:, :Q_LORA], gq_ref[...])
    ckv = _rms(a[:, Q_LORA:Q_LORA + KV_LORA], gkv_ref[...])

    half = LANES // 2
    nfreq = QK_ROPE // 2
    pack = LANES // nfreq
    blk = x_ref.shape[0] // pack
    lane = lax.broadcasted_iota(jnp.int32, (blk, LANES), 1)
    pos_p = pos_ref[(pack - 1) * blk:pack * blk, :]
    for b in range(pack - 2, -1, -1):
        pos_p = jnp.where(lane < (b + 1) * nfreq, pos_ref[b * blk:(b + 1) * blk, :], pos_p)
    ang = pos_p.astype(F32) * inv_ref[...]
    cos_p = jnp.cos(ang)
    sin_p = jnp.sin(ang)
    ca_blocks, sa_blocks = [], []
    for b in range(pack):
        shift = (LANES - b * nfreq) % LANES
        c0 = pltpu.roll(cos_p, shift, 1) if shift else cos_p
        s0 = pltpu.roll(sin_p, shift, 1) if shift else sin_p
        c1 = pltpu.roll(c0, nfreq, 1)
        s1 = pltpu.roll(s0, nfreq, 1)
        ca_blocks.append(jnp.where(lane < nfreq, c0, jnp.where(lane < 2 * nfreq, c1, 0.0)))
        sa_blocks.append(jnp.where(lane < nfreq, -s0, jnp.where(lane < 2 * nfreq, s1, 0.0)))
    ca = jnp.concatenate(ca_blocks, axis=0)
    sa = jnp.concatenate(sa_blocks, axis=0)

    kr_in = a[:, Q_LORA + KV_LORA:]
    kr = (kr_in * ca + pltpu.roll(kr_in, half, 1) * sa).astype(BF16)
    kn = jnp.dot(ckv.astype(BF16), wk_ref[...], preferred_element_type=F32)
    for h in range(MLA_HEADS):
        k_ref[h, :, 0:LANES] = kn[:, h * LANES:(h + 1) * LANES].astype(BF16)
        k_ref[h, :, LANES:HEAD_PAD] = kr

    ca_t = ca.T
    sa_t = sa.T
    scale = (QK_NOPE + QK_ROPE) ** -0.5 * math.log2(math.e)
    qt = jnp.dot(wqt_ref[...], cq.T.astype(BF16), preferred_element_type=F32)
    for h in range(MLA_HEADS):
        base = h * HEAD_PAD
        qt_ref[h, 0, 0:LANES, :] = (qt[base:base + LANES] * scale).astype(BF16)
        r_in = qt[base + LANES:base + HEAD_PAD]
        r_sw = jnp.concatenate([r_in[half:], r_in[:half]], axis=0)
        qt_ref[h, 0, LANES:HEAD_PAD, :] = ((r_in * ca_t + r_sw * sa_t) * scale).astype(BF16)
    vt = jnp.dot(wvt_ref[...], ckv.T.astype(BF16), preferred_element_type=F32)
    for h in range(MLA_HEADS):
        vt_ref[h, 0] = vt[h * V_HEAD:(h + 1) * V_HEAD].astype(BF16)


def _mla_proj(x, pos, g_mix, w_a, g_q, g_kv, w_uq, w_ukv):
    S = x.shape[0]
    H = MLA_HEADS
    ts = PROJ_TS
    kr = w_a[:, Q_LORA + KV_LORA:]
    kr_sw = jnp.concatenate([kr[:, QK_ROPE // 2:], kr[:, :QK_ROPE // 2]], axis=1)
    wa_ext = jnp.concatenate([w_a, kr_sw], axis=1).astype(BF16)
    wq = w_uq.reshape(Q_LORA, H, QK_NOPE + QK_ROPE)
    wq_r = wq[:, :, QK_NOPE:]
    wq_sw = jnp.concatenate([wq_r[:, :, QK_ROPE // 2:], wq_r[:, :, :QK_ROPE // 2]], axis=2)
    wq_t = jnp.concatenate([wq, wq_sw], axis=2).reshape(Q_LORA, H * HEAD_PAD).T.astype(BF16)
    wkv = w_ukv.reshape(KV_LORA, H, QK_NOPE + V_HEAD)
    wk = wkv[:, :, :QK_NOPE].reshape(KV_LORA, H * QK_NOPE).astype(BF16)
    wv_t = wkv[:, :, QK_NOPE:].reshape(KV_LORA, H * V_HEAD).T.astype(BF16)
    inv = 1.0 / (ROPE_THETA ** (jnp.arange(0, QK_ROPE, 2, dtype=F32) / QK_ROPE))
    inv_row = jnp.tile(inv, LANES // inv.shape[0])[None, :]

    n = S // ts
    per_q = ATT_TQ // ts
    per_kv = ATT_TK // ts
    row = lambda i: (i, 0)
    return pl.pallas_call(
        _mla_proj_kernel,
        grid=(n,),
        in_specs=[
            pl.BlockSpec((ts, D_MODEL), row),
            pl.BlockSpec((ts, 1), row),
            _const_spec((1, D_MODEL)),
            _const_spec(wa_ext.shape),
            _const_spec((1, Q_LORA)),
            _const_spec((1, KV_LORA)),
            _const_spec(wq_t.shape),
            _const_spec(wk.shape),
            _const_spec(wv_t.shape),
            _const_spec((1, LANES)),
        ],
        out_specs=[
            pl.BlockSpec((H, 1, HEAD_PAD, ts), lambda i: (0, i // per_q, 0, i % per_q)),
            pl.BlockSpec((H, ts, HEAD_PAD), lambda i: (0, i, 0)),
            pl.BlockSpec((H, 1, V_HEAD, ts), lambda i: (0, i // per_kv, 0, i % per_kv)),
        ],
        out_shape=[
            jax.ShapeDtypeStruct((H, S // ATT_TQ, HEAD_PAD, ATT_TQ), BF16),
            jax.ShapeDtypeStruct((H, S, HEAD_PAD), BF16),
            jax.ShapeDtypeStruct((H, S // ATT_TK, V_HEAD, ATT_TK), BF16),
        ],
        compiler_params=_params("parallel"),
        name="mla_proj",
    )(x, pos, g_mix[None, :], wa_ext, g_q[None, :], g_kv[None, :], wq_t, wk, wv_t, inv_row)


def _attn_kernel(qt_ref, k_ref, vt_ref, o_ref, sa_ref, sb_ref, m_ref, l_ref, acc_ref):
    i = pl.program_id(1)
    m_ref[...] = jnp.full(m_ref.shape, NEG_BIG, F32)
    l_ref[...] = jnp.zeros(l_ref.shape, F32)
    acc_ref[...] = jnp.zeros(acc_ref.shape, F32)
    groups = ATT_TK // SUBLANES

    def scores(j, dst):
        ks = pl.multiple_of(j * ATT_TK, ATT_TK)
        dst[...] = jnp.dot(k_ref[0, pl.ds(ks, ATT_TK), :], qt_ref[0, 0],
                           preferred_element_type=F32)

    def softmax_pv(src, j, masked):
        s = src[...]
        if masked:
            ck = lax.broadcasted_iota(jnp.int32, s.shape, 0) // CHUNK
            rq = (q_off + lax.broadcasted_iota(jnp.int32, s.shape, 1)) // CHUNK
            s = jnp.where(ck <= rq, s, NEG_BIG)
        m_prev = m_ref[...]
        m8 = jnp.max(s.reshape(groups, SUBLANES, ATT_TQ), axis=0)
        m_new = jnp.maximum(m_prev, jnp.max(m8, axis=0, keepdims=True))
        alpha = jnp.exp2(m_prev - m_new)
        p = jnp.exp2(s - m_new)
        l_ref[...] = alpha * l_ref[...] + jnp.sum(p.reshape(groups, SUBLANES, ATT_TQ), axis=0)
        pv = jnp.dot(vt_ref[0, j], p.astype(BF16), preferred_element_type=F32)
        acc_ref[...] = acc_ref[...] * alpha + pv
        m_ref[...] = m_new

    kv_per_q = ATT_TK // ATT_TQ
    n_full = i // kv_per_q
    q_off = (i % kv_per_q) * ATT_TQ

    scores(0, sa_ref)

    def pair(t, carry):
        j = 2 * t
        scores(j + 1, sb_ref)
        softmax_pv(sa_ref, j, False)
        scores(j + 2, sa_ref)
        softmax_pv(sb_ref, j + 1, False)
        return carry

    lax.fori_loop(0, n_full // 2, pair, 0)

    @pl.when(n_full % 2 == 0)
    def _():
        softmax_pv(sa_ref, n_full, True)

    @pl.when(n_full % 2 == 1)
    def _():
        scores(n_full, sb_ref)
        softmax_pv(sa_ref, n_full - 1, False)
        softmax_pv(sb_ref, n_full, True)

    l = jnp.sum(l_ref[...], axis=0, keepdims=True)
    o_ref[...] = (acc_ref[...] / l).T.astype(o_ref.dtype)


def _attention(qt, k, vt):
    H, S, _ = k.shape
    nq = S // ATT_TQ
    nk = S // ATT_TK
    return pl.pallas_call(
        _attn_kernel,
        grid=(H, nq),
        in_specs=[
            pl.BlockSpec((1, 1, HEAD_PAD, ATT_TQ), lambda h, i: (h, i, 0, 0)),
            pl.BlockSpec((1, S, HEAD_PAD), lambda h, i: (h, 0, 0)),
            pl.BlockSpec((1, nk, V_HEAD, ATT_TK), lambda h, i: (h, 0, 0, 0)),
        ],
        out_specs=pl.BlockSpec((ATT_TQ, V_HEAD), lambda h, i: (i, h)),
        out_shape=jax.ShapeDtypeStruct((S, H * V_HEAD), BF16),
        scratch_shapes=[
            pltpu.VMEM((ATT_TK, ATT_TQ), F32),
            pltpu.VMEM((ATT_TK, ATT_TQ), F32),
            pltpu.VMEM((1, ATT_TQ), F32),
            pltpu.VMEM((SUBLANES, ATT_TQ), F32),
            pltpu.VMEM((V_HEAD, ATT_TQ), F32),
        ],
        compiler_params=_params("parallel", "arbitrary"),
        name="mla_attention",
    )(qt, k, vt)


def _ffn_body(h, i, g_ref, wup_ref, cw_ref, cb_ref, wdn_ref, act_ref, tail_ref):
    ts = h.shape[0]
    hn = _rms(h, g_ref[...]).astype(BF16)

    @pl.when(i == 0)
    def _():
        tail_ref[...] = jnp.zeros(tail_ref.shape, F32)

    def conv_up(col):
        cols = slice(col, col + FFN_CF)
        up = jnp.dot(hn, wup_ref[:, cols], preferred_element_type=F32)
        ext = jnp.concatenate([tail_ref[:, cols], up], axis=0)
        tail_ref[:, cols] = up[ts - FFN_TAIL:, :]
        w = cw_ref[:, cols]
        return (ext[FFN_TAIL - 2:FFN_TAIL - 2 + ts] * w[0:1]
                + ext[FFN_TAIL - 1:FFN_TAIL - 1 + ts] * w[1:2]
                + up * w[2:3] + cb_ref[:, cols])

    for c in range(D_FF // FFN_CF):
        val = conv_up(c * FFN_CF)
        gate = conv_up(D_FF + c * FFN_CF)
        act_ref[:, c * FFN_CF:(c + 1) * FFN_CF] = (gate * jax.nn.sigmoid(gate) * val).astype(BF16)

    return h + jnp.dot(act_ref[...], wdn_ref[...], preferred_element_type=F32)


def _ffn0_kernel(x_ref, o_ref, wo_ref, g_ref, wup_ref, cw_ref, cb_ref, wdn_ref, gmix_ref,
                 win_ref, h_out_ref, u_out_ref, act_ref, tail_ref):
    i = pl.program_id(0)
    h1 = x_ref[...] + jnp.dot(o_ref[...], wo_ref[...], preferred_element_type=F32)
    h2 = _ffn_body(h1, i, g_ref, wup_ref, cw_ref, cb_ref, wdn_ref, act_ref, tail_ref)
    h_out_ref[...] = h2
    u_out_ref[...] = jnp.dot(_rms(h2, gmix_ref[...]).astype(BF16), win_ref[...],
                             preferred_element_type=F32)


def _ffn1_kernel(h_ref, y_ref, u_ref, d_ref, wglu_ref, g_ref, wup_ref, cw_ref, cb_ref,
                 wdn_ref, gfin_ref, out_ref, act_ref, tail_ref):
    i = pl.program_id(0)
    yy = jax.nn.gelu(y_ref[...] + d_ref[...] * u_ref[...])
    z = jnp.dot(yy.astype(BF16), wglu_ref[...], preferred_element_type=F32)
    h3 = h_ref[...] + z[:, :D_MODEL] * jax.nn.sigmoid(z[:, D_MODEL:])
    out = _ffn_body(h3, i, g_ref, wup_ref, cw_ref, cb_ref, wdn_ref, act_ref, tail_ref)
    out_ref[...] = _rms(out, gfin_ref[...])


def _layer_spec(stacked, layer):
    return pl.BlockSpec((None,) + stacked.shape[1:], lambda *_: (layer, 0, 0),
                        pipeline_mode=pl.Buffered(1))


def _ffn_specs(w_up_all, conv_w, w_down_all, layer):
    return [_const_spec((1, D_MODEL)), _layer_spec(w_up_all, layer), _const_spec(conv_w.shape),
            _const_spec((1, 2 * D_FF)), _layer_spec(w_down_all, layer)]


def _ffn_scratch():
    return [pltpu.VMEM((FFN_TS, D_FF), BF16), pltpu.VMEM((FFN_TAIL, 2 * D_FF), F32)]


def _attn_out_ffn_s5_in(x, o, w_o, g_ffn, w_up_all, conv_w, conv_b, w_down_all, g_mix, w_in):
    S = x.shape[0]
    row = lambda i: (i, 0)
    tile = pl.BlockSpec((FFN_TS, D_MODEL), row)
    return pl.pallas_call(
        _ffn0_kernel,
        grid=(S // FFN_TS,),
        in_specs=[tile, tile, _const_spec(w_o.shape)] + _ffn_specs(w_up_all, conv_w, w_down_all, 0)
                 + [_const_spec((1, D_MODEL)), _const_spec(w_in.shape)],
        out_specs=[tile, tile],
        out_shape=[jax.ShapeDtypeStruct((S, D_MODEL), F32)] * 2,
        scratch_shapes=_ffn_scratch(),
        compiler_params=_params("arbitrary"),
        name="conv_ffn0",
    )(x, o, w_o.astype(BF16), g_ffn[None, :], w_up_all, conv_w, conv_b[None, :],
      w_down_all, g_mix[None, :], w_in.astype(BF16))


def _s5_glu_ffn_final(h, y, u, d, w_glu, g_ffn, w_up_all, conv_w, conv_b, w_down_all, g_final):
    S = h.shape[0]
    row = lambda i: (i, 0)
    tile = pl.BlockSpec((FFN_TS, D_MODEL), row)
    return pl.pallas_call(
        _ffn1_kernel,
        grid=(S // FFN_TS,),
        in_specs=[tile, tile, tile, _const_spec((1, D_MODEL)), _const_spec(w_glu.shape)]
                 + _ffn_specs(w_up_all, conv_w, w_down_all, 1) + [_const_spec((1, D_MODEL))],
        out_specs=tile,
        out_shape=jax.ShapeDtypeStruct((S, D_MODEL), F32),
        scratch_shapes=_ffn_scratch(),
        compiler_params=_params("arbitrary"),
        name="conv_ffn1",
    )(h, y, u, d[None, :], w_glu.astype(BF16), g_ffn[None, :], w_up_all, conv_w,
      conv_b[None, :], w_down_all, g_final[None, :])


def _s5_kernel(ulo_ref, uhi_ref, bb_ref, cc_ref, a_ref, lp_ref, w_ref, y_ref,
               up_ref, buf_ref, xb_ref, yp_ref, carry_ref):
    t = pl.program_id(1)
    nc = S5_ST // LANES
    pair = 2 * S5_ROWS

    @pl.when(t == 0)
    def _():
        carry_ref[...] = jnp.zeros(carry_ref.shape, F32)

    per_sub = S5_L // S5_ROWS
    ch_rows = S5_CH * S5_ROWS
    grp_per_ch = S5_CH // S5_ROWS
    halves = S5_BLK // LANES

    def chunk_groups(k):
        for m in range(k * grp_per_ch, (k + 1) * grp_per_ch):
            for r in range(S5_ROWS):
                yield r * per_sub + m, m * S5_ROWS * S5_ROWS + r

    def col(ref_row, c):
        return jnp.broadcast_to(ref_row[:, c * LANES:(c + 1) * LANES], (S5_ROWS, LANES))

    ar = [col(a_ref[0, 0:1, :], c) for c in range(nc)]
    ai = [col(a_ref[0, 1:2, :], c) for c in range(nc)]

    def rows(l):
        return slice(l * S5_ROWS, (l + 1) * S5_ROWS)

    def re_cols(c):
        return slice(c * LANES, (c + 1) * LANES)

    def im_cols(c):
        return slice(S5_ST + c * LANES, S5_ST + (c + 1) * LANES)

    x = [jnp.zeros((S5_ROWS, LANES), F32)] * (2 * nc)
    for k in range(S5_L // S5_CH):
        for q, start in chunk_groups(k):
            for h, src in enumerate((ulo_ref, uhi_ref)):
                up_ref[h, pl.ds(start, S5_ROWS, stride=S5_ROWS), :] = (
                    src[q * S5_ROWS:(q + 1) * S5_ROWS, :])
        ck = slice(k * ch_rows, (k + 1) * ch_rows)
        up = jnp.concatenate([up_ref[h, ck, :] for h in range(halves)], axis=1).astype(BF16)
        buf_ref[ck, :] = jnp.dot(up, bb_ref[0], preferred_element_type=F32)
        for l in range(k * S5_CH, (k + 1) * S5_CH):
            for c in range(nc):
                xr, xi = x[2 * c], x[2 * c + 1]
                nr = ar[c] * xr - ai[c] * xi + buf_ref[rows(l), re_cols(c)]
                ni = ar[c] * xi + ai[c] * xr + buf_ref[rows(l), im_cols(c)]
                buf_ref[rows(l), re_cols(c)] = nr
                buf_ref[rows(l), im_cols(c)] = ni
                x[2 * c], x[2 * c + 1] = nr, ni
    ends = x

    sub = lax.broadcasted_iota(jnp.int32, (S5_ROWS, LANES), 0)
    prev = []
    for c in range(nc):
        sl = slice(c * LANES, (c + 1) * LANES)
        er, ei = ends[2 * c], ends[2 * c + 1]
        d = 1
        k = 0
        while d < S5_ROWS:
            pr = lp_ref[0, k, 0:1, sl]
            pi = lp_ref[0, k, 1:2, sl]
            sr = jnp.where(sub >= d, pltpu.roll(er, d, 0), 0.0)
            si = jnp.where(sub >= d, pltpu.roll(ei, d, 0), 0.0)
            er, ei = er + pr * sr - pi * si, ei + pr * si + pi * sr
            d *= 2
            k += 1
        cr = carry_ref[0:1, sl]
        ci = carry_ref[1:2, sl]
        wr = w_ref[0, 0, :, sl]
        wi = w_ref[0, 1, :, sl]
        er, ei = er + wr * cr - wi * ci, ei + wr * ci + wi * cr
        carry_ref[0:1, sl] = er[S5_ROWS - 1:S5_ROWS, :]
        carry_ref[1:2, sl] = ei[S5_ROWS - 1:S5_ROWS, :]
        prev += [jnp.where(sub >= 1, pltpu.roll(er, 1, 0), jnp.broadcast_to(cr, (S5_ROWS, LANES))),
                 jnp.where(sub >= 1, pltpu.roll(ei, 1, 0), jnp.broadcast_to(ci, (S5_ROWS, LANES)))]

    p = prev
    for k in range(S5_L // S5_CH):
        for j in range(k * S5_CH // 2, (k + 1) * S5_CH // 2):
            dst = slice(j * pair, (j + 1) * pair)
            for c in range(nc):
                p0r = ar[c] * p[2 * c] - ai[c] * p[2 * c + 1]
                p0i = ar[c] * p[2 * c + 1] + ai[c] * p[2 * c]
                p1r = ar[c] * p0r - ai[c] * p0i
                p1i = ar[c] * p0i + ai[c] * p0r
                xr = jnp.concatenate([buf_ref[rows(2 * j), re_cols(c)] + p0r,
                                      buf_ref[rows(2 * j + 1), re_cols(c)] + p1r], axis=0)
                xi = jnp.concatenate([buf_ref[rows(2 * j), im_cols(c)] + p0i,
                                      buf_ref[rows(2 * j + 1), im_cols(c)] + p1i], axis=0)
                xb_ref[dst, re_cols(c)] = xr.astype(BF16)
                xb_ref[dst, im_cols(c)] = xi.astype(BF16)
                p[2 * c], p[2 * c + 1] = p1r, p1i
        ck = slice(k * ch_rows, (k + 1) * ch_rows)
        yp = jnp.dot(xb_ref[ck, :], cc_ref[0], preferred_element_type=F32)
        for h in range(halves):
            yp_ref[h, ck, :] = yp[:, h * LANES:(h + 1) * LANES]
        for q, start in chunk_groups(k):
            for h in range(halves):
                y_ref[q * S5_ROWS:(q + 1) * S5_ROWS, h * LANES:(h + 1) * LANES] = (
                    yp_ref[h, pl.ds(start, S5_ROWS, stride=S5_ROWS), :])


def _s5_tables(lam_re, lam_im, log_dt, b_re, b_im, c_re, c_im):
    G, P, C = SSM_GROUPS, SSM_STATE, SSM_GROUP
    nb, gb = S5_NBLK, SSM_GROUPS // S5_NBLK
    dt = jnp.exp(log_dt.astype(F32))[:, None]
    lr = lam_re.astype(F32)
    li = lam_im.astype(F32)
    mag = jnp.exp(lr * dt)
    ar = mag * jnp.cos(li * dt)
    ai = mag * jnp.sin(li * dt)
    den = lr * lr + li * li
    nr = ar - 1.0
    coef_r = (nr * lr + ai * li) / den
    coef_i = (ai * lr - nr * li) / den
    br = b_re.astype(F32)
    bi = b_im.astype(F32)
    bbar_r = coef_r[..., None] * br - coef_i[..., None] * bi
    bbar_i = coef_r[..., None] * bi + coef_i[..., None] * br
    eye = jnp.eye(gb, dtype=F32)

    def blockdiag_in(m):
        return jnp.einsum('bgpc,gh->bgchp', m.reshape(nb, gb, P, C), eye).reshape(nb, gb * C, gb * P)

    def blockdiag_out(m):
        return jnp.einsum('bgcp,gh->bgphc', m.reshape(nb, gb, C, P), eye).reshape(nb, gb * P, gb * C)

    bb = jnp.concatenate([blockdiag_in(bbar_r), blockdiag_in(bbar_i)], axis=2).astype(BF16)
    cc = jnp.concatenate([blockdiag_out(c_re.astype(F32)),
                          -blockdiag_out(c_im.astype(F32))], axis=1).astype(BF16)

    def apow(nsteps):
        m = jnp.exp(lr * dt * nsteps)
        return m * jnp.cos(li * dt * nsteps), m * jnp.sin(li * dt * nsteps)

    def blk(x):
        lead = x.shape[:-2]
        x = x.reshape(lead + (nb, gb * P))
        return jnp.moveaxis(x, -2, 0)

    a_tab = jnp.stack([blk(ar), blk(ai)], axis=1)
    lp = []
    d = 1
    while d < S5_ROWS:
        pr, pi = apow(float(S5_L * d))
        lp.append(jnp.stack([blk(pr), blk(pi)], axis=1))
        d *= 2
    lp_tab = jnp.stack(lp, axis=1)
    steps = (S5_L * (jnp.arange(S5_ROWS, dtype=F32) + 1.0))[:, None, None]
    wr, wi = apow(steps)
    w_tab = jnp.stack([blk(wr), blk(wi)], axis=1)
    return bb, cc, a_tab, lp_tab, w_tab


def _s5_core(u, bb, cc, a_tab, lp_tab, w_tab):
    S = u.shape[0]
    nt = S // S5_TT
    return pl.pallas_call(
        _s5_kernel,
        grid=(S5_NBLK, nt),
        in_specs=[
            pl.BlockSpec((S5_TT, LANES), lambda b, t: (t, 2 * b)),
            pl.BlockSpec((S5_TT, LANES), lambda b, t: (t, 2 * b + 1)),
            pl.BlockSpec((1, S5_BLK, 2 * S5_ST), lambda b, t: (b, 0, 0)),
            pl.BlockSpec((1, 2 * S5_ST, S5_BLK), lambda b, t: (b, 0, 0)),
            pl.BlockSpec((1, 2, S5_ST), lambda b, t: (b, 0, 0)),
            pl.BlockSpec((1,) + lp_tab.shape[1:], lambda b, t: (b, 0, 0, 0)),
            pl.BlockSpec((1, 2, S5_ROWS, S5_ST), lambda b, t: (b, 0, 0, 0)),
        ],
        out_specs=pl.BlockSpec((S5_TT, S5_BLK), lambda b, t: (t, b)),
        out_shape=jax.ShapeDtypeStruct((S, D_MODEL), F32),
        scratch_shapes=[pltpu.VMEM((S5_BLK // LANES, S5_TT, LANES), F32),
                        pltpu.VMEM((S5_TT, 2 * S5_ST), F32),
                        pltpu.VMEM((S5_TT, 2 * S5_ST), BF16),
                        pltpu.VMEM((S5_BLK // LANES, S5_TT, LANES), F32),
                        pltpu.VMEM((2, S5_ST), F32)],
        compiler_params=_params("parallel", "arbitrary"),
        name="s5_core",
    )(u, u, bb, cc, a_tab, lp_tab, w_tab)


def kernel(x, positions, mla_w_a, mla_g_q, mla_g_kv, mla_w_uq, mla_w_ukv, mla_w_o,
           ssm_w_in, ssm_lambda_re, ssm_lambda_im, ssm_log_dt, ssm_b_re, ssm_b_im,
           ssm_c_re, ssm_c_im, ssm_d, ssm_w_glu, ffn_w_up, ffn_conv_w, ffn_conv_b,
           ffn_w_down, g_mix, g_ffn, g_final):
    B, S, D = x.shape
    assert (B, S, D) == (1, SEQ, D_MODEL)
    h = x.reshape(S, D)
    pos = positions.reshape(S, 1)

    qt, k, vt = _mla_proj(h, pos, g_mix[0], mla_w_a[0], mla_g_q[0], mla_g_kv[0],
                         mla_w_uq[0], mla_w_ukv[0])
    o = _attention(qt, k, vt)
    w_up_all = ffn_w_up.astype(BF16)
    w_down_all = ffn_w_down.astype(BF16)
    h, u = _attn_out_ffn_s5_in(h, o, mla_w_o[0], g_ffn[0], w_up_all, ffn_conv_w[0],
                               ffn_conv_b[0], w_down_all, g_mix[1], ssm_w_in[0])

    tabs = _s5_tables(ssm_lambda_re[0], ssm_lambda_im[0], ssm_log_dt[0],
                      ssm_b_re[0], ssm_b_im[0], ssm_c_re[0], ssm_c_im[0])
    y = _s5_core(u, *tabs)
    h = _s5_glu_ffn_final(h, y, u, ssm_d[0], ssm_w_glu[0], g_ffn[1], w_up_all,
                          ffn_conv_w[1], ffn_conv_b[1], w_down_all, g_final)
    return h.reshape(B, S, D)
```

```python
import functools
import math

import jax
import jax.numpy as jnp
from jax import lax
from jax.experimental import pallas as pl
from jax.experimental.pallas import tpu as pltpu

F32 = jnp.float32
BF16 = jnp.bfloat16

D_MODEL = 1024
SEQ = 16384
EPS = 1e-6
CHUNK = 64
MLA_HEADS = 8
QK_NOPE = 128
QK_ROPE = 64
V_HEAD = 128
Q_LORA = 384
KV_LORA = 256
ROPE_THETA = 10000.0
SSM_GROUP = 16
SSM_GROUPS = 64
SSM_STATE = 64
D_FF = 2816
CONV_W = 3

LANES = 128
SUBLANES = 8
HEAD_PAD = 2 * LANES
BF16_ROWS = 16
VMEM_LIMIT = 56 * 1024 * 1024

PROJ_TS = 512
ATT_TQ = 1024
ATT_TK = 1024
FFN_TS = 512
FFN_TAIL = SUBLANES
FFN_CF = 256
S5_TT = 1024
S5_ROWS = SUBLANES
S5_L = S5_TT // S5_ROWS
S5_CH = 16
S5_BLK = 2 * LANES
S5_NBLK = D_MODEL // S5_BLK
S5_ST = S5_BLK // SSM_GROUP * SSM_STATE
V_EXT = V_HEAD + BF16_ROWS
NEG_BIG = -1e30


def _rms(x, g):
    return x * lax.rsqrt(jnp.mean(x * x, axis=-1, keepdims=True) + EPS) * g


def _const_spec(shape):
    nd = len(shape)
    return pl.BlockSpec(shape, lambda *_: (0,) * nd, pipeline_mode=pl.Buffered(1))


def _params(*sem):
    return pltpu.CompilerParams(dimension_semantics=sem, vmem_limit_bytes=VMEM_LIMIT)


def _mla_proj_kernel(x_ref, pos_ref, gmix_ref, wa_ref, gq_ref, gkv_ref, wqt_ref,
                     wk_ref, wvt_ref, inv_ref, qt_ref, k_ref, vt_ref):
    hn = _rms(x_ref[...], gmix_ref[...])
    a = jnp.dot(hn.astype(BF16), wa_ref[...], preferred_element_type=F32)
    cq = _rms(a[:, :Q_LORA], gq_ref[...])
    ckv = _rms(a[:, Q_LORA:Q_LORA + KV_LORA], gkv_ref[...])

    half = LANES // 2
    nfreq = QK_ROPE // 2
    pack = LANES // nfreq
    blk = x_ref.shape[0] // pack
    lane = lax.broadcasted_iota(jnp.int32, (blk, LANES), 1)
    pos_p = pos_ref[(pack - 1) * blk:pack * blk, :]
    for b in range(pack - 2, -1, -1):
        pos_p = jnp.where(lane < (b + 1) * nfreq, pos_ref[b * blk:(b + 1) * blk, :], pos_p)
    ang = pos_p.astype(F32) * inv_ref[...]
    cos_p = jnp.cos(ang)
    sin_p = jnp.sin(ang)
    ca_blocks, sa_blocks = [], []
    for b in range(pack):
        shift = (LANES - b * nfreq) % LANES
        c0 = pltpu.roll(cos_p, shift, 1) if shift else cos_p
        s0 = pltpu.roll(sin_p, shift, 1) if shift else sin_p
        c1 = pltpu.roll(c0, nfreq, 1)
        s1 = pltpu.roll(s0, nfreq, 1)
        ca_blocks.append(jnp.where(lane < nfreq, c0, jnp.where(lane < 2 * nfreq, c1, 0.0)))
        sa_blocks.append(jnp.where(lane < nfreq, -s0, jnp.where(lane < 2 * nfreq, s1, 0.0)))
    ca = jnp.concatenate(ca_blocks, axis=0)
    sa = jnp.concatenate(sa_blocks, axis=0)

    kr_in = a[:, Q_LORA + KV_LORA:]
    kr = (kr_in * ca + pltpu.roll(kr_in, half, 1) * sa).astype(BF16)
    kn = jnp.dot(ckv.astype(BF16), wk_ref[...], preferred_element_type=F32)
    for h in range(MLA_HEADS):
        k_ref[h, :, 0:LANES] = kn[:, h * LANES:(h + 1) * LANES].astype(BF16)
        k_ref[h, :, LANES:HEAD_PAD] = kr

    ca_t = ca.T
    sa_t = sa.T
    scale = (QK_NOPE + QK_ROPE) ** -0.5 * math.log2(math.e)
    qt = jnp.dot(wqt_ref[...], cq.T.astype(BF16), preferred_element_type=F32)
    for h in range(MLA_HEADS):
        base = h * HEAD_PAD
        qt_ref[h, 0, 0:LANES, :] = (qt[base:base + LANES] * scale).astype(BF16)
        r_in = qt[base + LANES:base + HEAD_PAD]
        r_sw = jnp.concatenate([r_in[half:], r_in[:half]], axis=0)
        qt_ref[h, 0, LANES:HEAD_PAD, :] = ((r_in * ca_t + r_sw * sa_t) * scale).astype(BF16)
    vt = jnp.dot(wvt_ref[...], ckv.T.astype(BF16), preferred_element_type=F32)
    ones_grp = jnp.where(lax.broadcasted_iota(jnp.int32, (V_EXT - V_HEAD, vt.shape[1]), 0) == 0,
                         1.0, 0.0).astype(BF16)
    for h in range(MLA_HEADS):
        vt_ref[h, 0, 0:V_HEAD, :] = vt[h * V_HEAD:(h + 1) * V_HEAD].astype(BF16)
        vt_ref[h, 0, V_HEAD:V_EXT, :] = ones_grp


def _mla_proj(x, pos, g_mix, w_a, g_q, g_kv, w_uq, w_ukv):
    S = x.shape[0]
    H = MLA_HEADS
    ts = PROJ_TS
    kr = w_a[:, Q_LORA + KV_LORA:]
    kr_sw = jnp.concatenate([kr[:, QK_ROPE // 2:], kr[:, :QK_ROPE // 2]], axis=1)
    wa_ext = jnp.concatenate([w_a, kr_sw], axis=1).astype(BF16)
    wq = w_uq.reshape(Q_LORA, H, QK_NOPE + QK_ROPE)
    wq_r = wq[:, :, QK_NOPE:]
    wq_sw = jnp.concatenate([wq_r[:, :, QK_ROPE // 2:], wq_r[:, :, :QK_ROPE // 2]], axis=2)
    wq_t = jnp.concatenate([wq, wq_sw], axis=2).reshape(Q_LORA, H * HEAD_PAD).T.astype(BF16)
    wkv = w_ukv.reshape(KV_LORA, H, QK_NOPE + V_HEAD)
    wk = wkv[:, :, :QK_NOPE].reshape(KV_LORA, H * QK_NOPE).astype(BF16)
    wv_t = wkv[:, :, QK_NOPE:].reshape(KV_LORA, H * V_HEAD).T.astype(BF16)
    inv = 1.0 / (ROPE_THETA ** (jnp.arange(0, QK_ROPE, 2, dtype=F32) / QK_ROPE))
    inv_row = jnp.tile(inv, LANES // inv.shape[0])[None, :]

    n = S // ts
    per_q = ATT_TQ // ts
    per_kv = ATT_TK // ts
    row = lambda i: (i, 0)
    return pl.pallas_call(
        _mla_proj_kernel,
        grid=(n,),
        in_specs=[
            pl.BlockSpec((ts, D_MODEL), row),
            pl.BlockSpec((ts, 1), row),
            _const_spec((1, D_MODEL)),
            _const_spec(wa_ext.shape),
            _const_spec((1, Q_LORA)),
            _const_spec((1, KV_LORA)),
            _const_spec(wq_t.shape),
            _const_spec(wk.shape),
            _const_spec(wv_t.shape),
            _const_spec((1, LANES)),
        ],
        out_specs=[
            pl.BlockSpec((H, 1, HEAD_PAD, ts), lambda i: (0, i // per_q, 0, i % per_q)),
            pl.BlockSpec((H, ts, HEAD_PAD), lambda i: (0, i, 0)),
            pl.BlockSpec((H, 1, V_EXT, ts), lambda i: (0, i // per_kv, 0, i % per_kv)),
        ],
        out_shape=[
            jax.ShapeDtypeStruct((H, S // ATT_TQ, HEAD_PAD, ATT_TQ), BF16),
            jax.ShapeDtypeStruct((H, S, HEAD_PAD), BF16),
            jax.ShapeDtypeStruct((H, S // ATT_TK, V_EXT, ATT_TK), BF16),
        ],
        compiler_params=_params("parallel"),
        name="mla_proj",
    )(x, pos, g_mix[None, :], wa_ext, g_q[None, :], g_kv[None, :], wq_t, wk, wv_t, inv_row)


def _attn_kernel(qt_ref, k_ref, vt_ref, o_ref, sa_ref, sb_ref, m_ref, acc_ref):
    i = pl.program_id(1)
    m_ref[...] = jnp.full(m_ref.shape, NEG_BIG, F32)
    acc_ref[...] = jnp.zeros(acc_ref.shape, F32)
    groups = ATT_TK // SUBLANES

    def scores(j, dst):
        ks = pl.multiple_of(j * ATT_TK, ATT_TK)
        dst[...] = jnp.dot(k_ref[0, pl.ds(ks, ATT_TK), :], qt_ref[0, 0],
                           preferred_element_type=F32)

    def softmax_pv(src, j, masked):
        s = src[...]
        if masked:
            ck = lax.broadcasted_iota(jnp.int32, s.shape, 0) // CHUNK
            rq = (q_off + lax.broadcasted_iota(jnp.int32, s.shape, 1)) // CHUNK
            s = jnp.where(ck <= rq, s, NEG_BIG)
        m_prev = m_ref[...]
        m8 = jnp.max(s.reshape(groups, SUBLANES, ATT_TQ), axis=0)
        m_new = jnp.maximum(m_prev, jnp.max(m8, axis=0, keepdims=True))
        alpha = jnp.exp2(m_prev - m_new)
        p = jnp.exp2(s - m_new)
        pv = jnp.dot(vt_ref[0, j], p.astype(BF16), preferred_element_type=F32)
        acc_ref[...] = acc_ref[...] * alpha + pv
        m_ref[...] = m_new

    kv_per_q = ATT_TK // ATT_TQ
    n_full = i // kv_per_q
    q_off = (i % kv_per_q) * ATT_TQ

    scores(0, sa_ref)

    def pair(t, carry):
        j = 2 * t
        scores(j + 1, sb_ref)
        softmax_pv(sa_ref, j, False)
        scores(j + 2, sa_ref)
        softmax_pv(sb_ref, j + 1, False)
        return carry

    lax.fori_loop(0, n_full // 2, pair, 0)

    @pl.when(n_full % 2 == 0)
    def _():
        softmax_pv(sa_ref, n_full, True)

    @pl.when(n_full % 2 == 1)
    def _():
        scores(n_full, sb_ref)
        softmax_pv(sa_ref, n_full - 1, False)
        softmax_pv(sb_ref, n_full, True)

    acc = acc_ref[...]
    o_ref[...] = (acc[:V_HEAD] / acc[V_HEAD:V_HEAD + 1]).T.astype(o_ref.dtype)


def _attention(qt, k, vt):
    H, S, _ = k.shape
    nq = S // ATT_TQ
    nk = S // ATT_TK
    return pl.pallas_call(
        _attn_kernel,
        grid=(H, nq),
        in_specs=[
            pl.BlockSpec((1, 1, HEAD_PAD, ATT_TQ), lambda h, i: (h, i, 0, 0)),
            pl.BlockSpec((1, S, HEAD_PAD), lambda h, i: (h, 0, 0)),
            pl.BlockSpec((1, nk, V_EXT, ATT_TK), lambda h, i: (h, 0, 0, 0)),
        ],
        out_specs=pl.BlockSpec((ATT_TQ, V_HEAD), lambda h, i: (i, h)),
        out_shape=jax.ShapeDtypeStruct((S, H * V_HEAD), BF16),
        scratch_shapes=[
            pltpu.VMEM((ATT_TK, ATT_TQ), F32),
            pltpu.VMEM((ATT_TK, ATT_TQ), F32),
            pltpu.VMEM((1, ATT_TQ), F32),
            pltpu.VMEM((V_EXT, ATT_TQ), F32),
        ],
        compiler_params=_params("parallel", "arbitrary"),
        name="mla_attention",
    )(qt, k, vt)


def _ffn_body(h, i, g_ref, wup_ref, cw_ref, cb_ref, wdn_ref, act_ref, tail_ref):
    ts = h.shape[0]
    hn = _rms(h, g_ref[...]).astype(BF16)

    @pl.when(i == 0)
    def _():
        tail_ref[...] = jnp.zeros(tail_ref.shape, F32)

    def conv_up(col):
        cols = slice(col, col + FFN_CF)
        up = jnp.dot(hn, wup_ref[:, cols], preferred_element_type=F32)
        ext = jnp.concatenate([tail_ref[:, cols], up], axis=0)
        tail_ref[:, cols] = up[ts - FFN_TAIL:, :]
        w = cw_ref[:, cols]
        return (ext[FFN_TAIL - 2:FFN_TAIL - 2 + ts] * w[0:1]
                + ext[FFN_TAIL - 1:FFN_TAIL - 1 + ts] * w[1:2]
                + up * w[2:3] + cb_ref[:, cols])

    for c in range(D_FF // FFN_CF):
        val = conv_up(c * FFN_CF)
        gate = conv_up(D_FF + c * FFN_CF)
        act_ref[:, c * FFN_CF:(c + 1) * FFN_CF] = (gate * jax.nn.sigmoid(gate) * val).astype(BF16)

    return h + jnp.dot(act_ref[...], wdn_ref[...], preferred_element_type=F32)


def _ffn0_kernel(x_ref, o_ref, wo_ref, g_ref, wup_ref, cw_ref, cb_ref, wdn_ref, gmix_ref,
                 win_ref, h_out_ref, u_out_ref, act_ref, tail_ref):
    i = pl.program_id(0)
    h1 = x_ref[...] + jnp.dot(o_ref[...], wo_ref[...], preferred_element_type=F32)
    h2 = _ffn_body(h1, i, g_ref, wup_ref, cw_ref, cb_ref, wdn_ref, act_ref, tail_ref)
    h_out_ref[...] = h2
    u_out_ref[...] = jnp.dot(_rms(h2, gmix_ref[...]).astype(BF16), win_ref[...],
                             preferred_element_type=F32)


def _ffn1_kernel(h_ref, y_ref, u_ref, d_ref, wglu_ref, g_ref, wup_ref, cw_ref, cb_ref,
                 wdn_ref, gfin_ref, out_ref, act_ref, tail_ref):
    i = pl.program_id(0)
    yy = jax.nn.gelu(y_ref[...] + d_ref[...] * u_ref[...])
    z = jnp.dot(yy.astype(BF16), wglu_ref[...], preferred_element_type=F32)
    h3 = h_ref[...] + z[:, :D_MODEL] * jax.nn.sigmoid(z[:, D_MODEL:])
    out = _ffn_body(h3, i, g_ref, wup_ref, cw_ref, cb_ref, wdn_ref, act_ref, tail_ref)
    out_ref[...] = _rms(out, gfin_ref[...])


def _layer_spec(stacked, layer):
    return pl.BlockSpec((None,) + stacked.shape[1:], lambda *_: (layer, 0, 0),
                        pipeline_mode=pl.Buffered(1))


def _ffn_specs(w_up_all, conv_w, w_down_all, layer):
    return [_const_spec((1, D_MODEL)), _layer_spec(w_up_all, layer), _const_spec(conv_w.shape),
            _const_spec((1, 2 * D_FF)), _layer_spec(w_down_all, layer)]


def _ffn_scratch():
    return [pltpu.VMEM((FFN_TS, D_FF), BF16), pltpu.VMEM((FFN_TAIL, 2 * D_FF), F32)]


def _attn_out_ffn_s5_in(x, o, w_o, g_ffn, w_up_all, conv_w, conv_b, w_down_all, g_mix, w_in):
    S = x.shape[0]
    row = lambda i: (i, 0)
    tile = pl.BlockSpec((FFN_TS, D_MODEL), row)
    return pl.pallas_call(
        _ffn0_kernel,
        grid=(S // FFN_TS,),
        in_specs=[tile, tile, _const_spec(w_o.shape)] + _ffn_specs(w_up_all, conv_w, w_down_all, 0)
                 + [_const_spec((1, D_MODEL)), _const_spec(w_in.shape)],
        out_specs=[tile, tile],
        out_shape=[jax.ShapeDtypeStruct((S, D_MODEL), F32)] * 2,
        scratch_shapes=_ffn_scratch(),
        compiler_params=_params("arbitrary"),
        name="conv_ffn0",
    )(x, o, w_o.astype(BF16), g_ffn[None, :], w_up_all, conv_w, conv_b[None, :],
      w_down_all, g_mix[None, :], w_in.astype(BF16))


def _s5_glu_ffn_final(h, y, u, d, w_glu, g_ffn, w_up_all, conv_w, conv_b, w_down_all, g_final):
    S = h.shape[0]
    row = lambda i: (i, 0)
    tile = pl.BlockSpec((FFN_TS, D_MODEL), row)
    return pl.pallas_call(
        _ffn1_kernel,
        grid=(S // FFN_TS,),
        in_specs=[tile, tile, tile, _const_spec((1, D_MODEL)), _const_spec(w_glu.shape)]
                 + _ffn_specs(w_up_all, conv_w, w_down_all, 1) + [_const_spec((1, D_MODEL))],
        out_specs=tile,
        out_shape=jax.ShapeDtypeStruct((S, D_MODEL), F32),
        scratch_shapes=_ffn_scratch(),
        compiler_params=_params("arbitrary"),
        name="conv_ffn1",
    )(h, y, u, d[None, :], w_glu.astype(BF16), g_ffn[None, :], w_up_all, conv_w,
      conv_b[None, :], w_down_all, g_final[None, :])


def _s5_kernel(ulo_ref, uhi_ref, bb_ref, cc_ref, a_ref, lp_ref, w_ref, y_ref,
               up_ref, buf_ref, xb_ref, yp_ref, carry_ref):
    t = pl.program_id(1)
    nc = S5_ST // LANES
    pair = 2 * S5_ROWS

    @pl.when(t == 0)
    def _():
        carry_ref[...] = jnp.zeros(carry_ref.shape, F32)

    per_sub = S5_L // S5_ROWS
    ch_rows = S5_CH * S5_ROWS
    grp_per_ch = S5_CH // S5_ROWS
    halves = S5_BLK // LANES

    def chunk_groups(k):
        for m in range(k * grp_per_ch, (k + 1) * grp_per_ch):
            for r in range(S5_ROWS):
                yield r * per_sub + m, m * S5_ROWS * S5_ROWS + r

    def col(ref_row, c):
        return jnp.broadcast_to(ref_row[:, c * LANES:(c + 1) * LANES], (S5_ROWS, LANES))

    ar = [col(a_ref[0, 0:1, :], c) for c in range(nc)]
    ai = [col(a_ref[0, 1:2, :], c) for c in range(nc)]

    def rows(l):
        return slice(l * S5_ROWS, (l + 1) * S5_ROWS)

    def re_cols(c):
        return slice(c * LANES, (c + 1) * LANES)

    def im_cols(c):
        return slice(S5_ST + c * LANES, S5_ST + (c + 1) * LANES)

    x = [jnp.zeros((S5_ROWS, LANES), F32)] * (2 * nc)
    for k in range(S5_L // S5_CH):
        for q, start in chunk_groups(k):
            for h, src in enumerate((ulo_ref, uhi_ref)):
                up_ref[h, pl.ds(start, S5_ROWS, stride=S5_ROWS), :] = (
                    src[q * S5_ROWS:(q + 1) * S5_ROWS, :])
        ck = slice(k * ch_rows, (k + 1) * ch_rows)
        up = jnp.concatenate([up_ref[h, ck, :] for h in range(halves)], axis=1).astype(BF16)
        buf_ref[ck, :] = jnp.dot(up, bb_ref[0], preferred_element_type=F32)
        for l in range(k * S5_CH, (k + 1) * S5_CH):
            for c in range(nc):
                xr, xi = x[2 * c], x[2 * c + 1]
                nr = ar[c] * xr - ai[c] * xi + buf_ref[rows(l), re_cols(c)]
                ni = ar[c] * xi + ai[c] * xr + buf_ref[rows(l), im_cols(c)]
                buf_ref[rows(l), re_cols(c)] = nr
                buf_ref[rows(l), im_cols(c)] = ni
                x[2 * c], x[2 * c + 1] = nr, ni
    ends = x

    sub = lax.broadcasted_iota(jnp.int32, (S5_ROWS, LANES), 0)
    prev = []
    for c in range(nc):
        sl = slice(c * LANES, (c + 1) * LANES)
        er, ei = ends[2 * c], ends[2 * c + 1]
        d = 1
        k = 0
        while d < S5_ROWS:
            pr = lp_ref[0, k, 0:1, sl]
            pi = lp_ref[0, k, 1:2, sl]
            sr = jnp.where(sub >= d, pltpu.roll(er, d, 0), 0.0)
            si = jnp.where(sub >= d, pltpu.roll(ei, d, 0), 0.0)
            er, ei = er + pr * sr - pi * si, ei + pr * si + pi * sr
            d *= 2
            k += 1
        cr = carry_ref[0:1, sl]
        ci = carry_ref[1:2, sl]
        wr = w_ref[0, 0, :, sl]
        wi = w_ref[0, 1, :, sl]
        er, ei = er + wr * cr - wi * ci, ei + wr * ci + wi * cr
        carry_ref[0:1, sl] = er[S5_ROWS - 1:S5_ROWS, :]
        carry_ref[1:2, sl] = ei[S5_ROWS - 1:S5_ROWS, :]
        prev += [jnp.where(sub >= 1, pltpu.roll(er, 1, 0), jnp.broadcast_to(cr, (S5_ROWS, LANES))),
                 jnp.where(sub >= 1, pltpu.roll(ei, 1, 0), jnp.broadcast_to(ci, (S5_ROWS, LANES)))]

    p = prev
    for k in range(S5_L // S5_CH):
        for j in range(k * S5_CH // 2, (k + 1) * S5_CH // 2):
            dst = slice(j * pair, (j + 1) * pair)
            for c in range(nc):
                p0r = ar[c] * p[2 * c] - ai[c] * p[2 * c + 1]
                p0i = ar[c] * p[2 * c + 1] + ai[c] * p[2 * c]
                p1r = ar[c] * p0r - ai[c] * p0i
                p1i = ar[c] * p0i + ai[c] * p0r
                xr = jnp.concatenate([buf_ref[rows(2 * j), re_cols(c)] + p0r,
                                      buf_ref[rows(2 * j + 1), re_cols(c)] + p1r], axis=0)
                xi = jnp.concatenate([buf_ref[rows(2 * j), im_cols(c)] + p0i,
                                      buf_ref[rows(2 * j + 1), im_cols(c)] + p1i], axis=0)
                xb_ref[dst, re_cols(c)] = xr.astype(BF16)
                xb_ref[dst, im_cols(c)] = xi.astype(BF16)
                p[2 * c], p[2 * c + 1] = p1r, p1i
        ck = slice(k * ch_rows, (k + 1) * ch_rows)
        yp = jnp.dot(xb_ref[ck, :], cc_ref[0], preferred_element_type=F32)
        for h in range(halves):
            yp_ref[h, ck, :] = yp[:, h * LANES:(h + 1) * LANES]
        for q, start in chunk_groups(k):
            for h in range(halves):
                y_ref[q * S5_ROWS:(q + 1) * S5_ROWS, h * LANES:(h + 1) * LANES] = (
                    yp_ref[h, pl.ds(start, S5_ROWS, stride=S5_ROWS), :])


def _s5_tables(lam_re, lam_im, log_dt, b_re, b_im, c_re, c_im):
    G, P, C = SSM_GROUPS, SSM_STATE, SSM_GROUP
    nb, gb = S5_NBLK, SSM_GROUPS // S5_NBLK
    dt = jnp.exp(log_dt.astype(F32))[:, None]
    lr = lam_re.astype(F32)
    li = lam_im.astype(F32)
    mag = jnp.exp(lr * dt)
    ar = mag * jnp.cos(li * dt)
    ai = mag * jnp.sin(li * dt)
    den = lr * lr + li * li
    nr = ar - 1.0
    coef_r = (nr * lr + ai * li) / den
    coef_i = (ai * lr - nr * li) / den
    br = b_re.astype(F32)
    bi = b_im.astype(F32)
    bbar_r = coef_r[..., None] * br - coef_i[..., None] * bi
    bbar_i = coef_r[..., None] * bi + coef_i[..., None] * br
    eye = jnp.eye(gb, dtype=F32)

    def blockdiag_in(m):
        return jnp.einsum('bgpc,gh->bgchp', m.reshape(nb, gb, P, C), eye).reshape(nb, gb * C, gb * P)

    def blockdiag_out(m):
        return jnp.einsum('bgcp,gh->bgphc', m.reshape(nb, gb, C, P), eye).reshape(nb, gb * P, gb * C)

    bb = jnp.concatenate([blockdiag_in(bbar_r), blockdiag_in(bbar_i)], axis=2).astype(BF16)
    cc = jnp.concatenate([blockdiag_out(c_re.astype(F32)),
                          -blockdiag_out(c_im.astype(F32))], axis=1).astype(BF16)

    def apow(nsteps):
        m = jnp.exp(lr * dt * nsteps)
        return m * jnp.cos(li * dt * nsteps), m * jnp.sin(li * dt * nsteps)

    def blk(x):
        lead = x.shape[:-2]
        x = x.reshape(lead + (nb, gb * P))
        return jnp.moveaxis(x, -2, 0)

    a_tab = jnp.stack([blk(ar), blk(ai)], axis=1)
    lp = []
    d = 1
    while d < S5_ROWS:
        pr, pi = apow(float(S5_L * d))
        lp.append(jnp.stack([blk(pr), blk(pi)], axis=1))
        d *= 2
    lp_tab = jnp.stack(lp, axis=1)
    steps = (S5_L * (jnp.arange(S5_ROWS, dtype=F32) + 1.0))[:, None, None]
    wr, wi = apow(steps)
    w_tab = jnp.stack([blk(wr), blk(wi)], axis=1)
    return bb, cc, a_tab, lp_tab, w_tab


def _s5_core(u, bb, cc, a_tab, lp_tab, w_tab):
    S = u.shape[0]
    nt = S // S5_TT
    return pl.pallas_call(
        _s5_kernel,
        grid=(S5_NBLK, nt),
        in_specs=[
            pl.BlockSpec((S5_TT, LANES), lambda b, t: (t, 2 * b)),
            pl.BlockSpec((S5_TT, LANES), lambda b, t: (t, 2 * b + 1)),
            pl.BlockSpec((1, S5_BLK, 2 * S5_ST), lambda b, t: (b, 0, 0)),
            pl.BlockSpec((1, 2 * S5_ST, S5_BLK), lambda b, t: (b, 0, 0)),
            pl.BlockSpec((1, 2, S5_ST), lambda b, t: (b, 0, 0)),
            pl.BlockSpec((1,) + lp_tab.shape[1:], lambda b, t: (b, 0, 0, 0)),
            pl.BlockSpec((1, 2, S5_ROWS, S5_ST), lambda b, t: (b, 0, 0, 0)),
        ],
        out_specs=pl.BlockSpec((S5_TT, S5_BLK), lambda b, t: (t, b)),
        out_shape=jax.ShapeDtypeStruct((S, D_MODEL), F32),
        scratch_shapes=[pltpu.VMEM((S5_BLK // LANES, S5_TT, LANES), F32),
                        pltpu.VMEM((S5_TT, 2 * S5_ST), F32),
                        pltpu.VMEM((S5_TT, 2 * S5_ST), BF16),
                        pltpu.VMEM((S5_BLK // LANES, S5_TT, LANES), F32),
                        pltpu.VMEM((2, S5_ST), F32)],
        compiler_params=_params("parallel", "arbitrary"),
        name="s5_core",
    )(u, u, bb, cc, a_tab, lp_tab, w_tab)


def kernel(x, positions, mla_w_a, mla_g_q, mla_g_kv, mla_w_uq, mla_w_ukv, mla_w_o,
           ssm_w_in, ssm_lambda_re, ssm_lambda_im, ssm_log_dt, ssm_b_re, ssm_b_im,
           ssm_c_re, ssm_c_im, ssm_d, ssm_w_glu, ffn_w_up, ffn_conv_w, ffn_conv_b,
           ffn_w_down, g_mix, g_ffn, g_final):
    B, S, D = x.shape
    assert (B, S, D) == (1, SEQ, D_MODEL)
    h = x.reshape(S, D)
    pos = positions.reshape(S, 1)

    qt, k, vt = _mla_proj(h, pos, g_mix[0], mla_w_a[0], mla_g_q[0], mla_g_kv[0],
                         mla_w_uq[0], mla_w_ukv[0])
    o = _attention(qt, k, vt)
    w_up_all = ffn_w_up.astype(BF16)
    w_down_all = ffn_w_down.astype(BF16)
    h, u = _attn_out_ffn_s5_in(h, o, mla_w_o[0], g_ffn[0], w_up_all, ffn_conv_w[0],
                               ffn_conv_b[0], w_down_all, g_mix[1], ssm_w_in[0])

    tabs = _s5_tables(ssm_lambda_re[0], ssm_lambda_im[0], ssm_log_dt[0],
                      ssm_b_re[0], ssm_b_im[0], ssm_c_re[0], ssm_c_im[0])
    y = _s5_core(u, *tabs)
    h = _s5_glu_ffn_final(h, y, u, ssm_d[0], ssm_w_glu[0], g_ffn[1], w_up_all,
                          ffn_conv_w[1], ffn_conv_b[1], w_down_all, g_final)
    return h.reshape(B, S, D)
```

```python
import functools
import math

import jax
import jax.numpy as jnp
from jax import lax
from jax.experimental import pallas as pl
from jax.experimental.pallas import tpu as pltpu

F32 = jnp.float32
BF16 = jnp.bfloat16

D_MODEL = 1024
SEQ = 16384
EPS = 1e-6
CHUNK = 64
MLA_HEADS = 8
QK_NOPE = 128
QK_ROPE = 64
V_HEAD = 128
Q_LORA = 384
KV_LORA = 256
ROPE_THETA = 10000.0
SSM_GROUP = 16
SSM_GROUPS = 64
SSM_STATE = 64
D_FF = 2816
CONV_W = 3

LANES = 128
SUBLANES = 8
HEAD_PAD = 2 * LANES
BF16_ROWS = 16
VMEM_LIMIT = 56 * 1024 * 1024

PROJ_TS = 512
ATT_TQ = 1024
ATT_TK = 1024
FFN_TS = 512
FFN_SUB = 256
FFN_TAIL = SUBLANES
FFN_CF = 256
S5_TT = 1024
S5_ROWS = SUBLANES
S5_L = S5_TT // S5_ROWS
S5_CH = 16
S5_BLK = 2 * LANES
S5_NBLK = D_MODEL // S5_BLK
S5_ST = S5_BLK // SSM_GROUP * SSM_STATE
V_EXT = V_HEAD + BF16_ROWS
NEG_BIG = -1e30


def _rms(x, g):
    return x * lax.rsqrt(jnp.mean(x * x, axis=-1, keepdims=True) + EPS) * g


def _const_spec(shape):
    nd = len(shape)
    return pl.BlockSpec(shape, lambda *_: (0,) * nd, pipeline_mode=pl.Buffered(1))


def _params(*sem):
    return pltpu.CompilerParams(dimension_semantics=sem, vmem_limit_bytes=VMEM_LIMIT)


def _mla_proj_kernel(x_ref, pos_ref, gmix_ref, wa_ref, gq_ref, gkv_ref, wqt_ref,
                     wk_ref, wvt_ref, inv_ref, qt_ref, k_ref, vt_ref):
    hn = _rms(x_ref[...], gmix_ref[...])
    a = jnp.dot(hn.astype(BF16), wa_ref[...], preferred_element_type=F32)
    cq = _rms(a[:, :Q_LORA], gq_ref[...])
    ckv = _rms(a[:, Q_LORA:Q_LORA + KV_LORA], gkv_ref[...])

    half = LANES // 2
    nfreq = QK_ROPE // 2
    pack = LANES // nfreq
    blk = x_ref.shape[0] // pack
    lane = lax.broadcasted_iota(jnp.int32, (blk, LANES), 1)
    pos_p = pos_ref[(pack - 1) * blk:pack * blk, :]
    for b in range(pack - 2, -1, -1):
        pos_p = jnp.where(lane < (b + 1) * nfreq, pos_ref[b * blk:(b + 1) * blk, :], pos_p)
    ang = pos_p.astype(F32) * inv_ref[...]
    cos_p = jnp.cos(ang)
    sin_p = jnp.sin(ang)
    ca_blocks, sa_blocks = [], []
    for b in range(pack):
        shift = (LANES - b * nfreq) % LANES
        c0 = pltpu.roll(cos_p, shift, 1) if shift else cos_p
        s0 = pltpu.roll(sin_p, shift, 1) if shift else sin_p
        c1 = pltpu.roll(c0, nfreq, 1)
        s1 = pltpu.roll(s0, nfreq, 1)
        ca_blocks.append(jnp.where(lane < nfreq, c0, jnp.where(lane < 2 * nfreq, c1, 0.0)))
        sa_blocks.append(jnp.where(lane < nfreq, -s0, jnp.where(lane < 2 * nfreq, s1, 0.0)))
    ca = jnp.concatenate(ca_blocks, axis=0)
    sa = jnp.concatenate(sa_blocks, axis=0)

    kr_in = a[:, Q_LORA + KV_LORA:]
    kr = (kr_in * ca + pltpu.roll(kr_in, half, 1) * sa).astype(BF16)
    kn = jnp.dot(ckv.astype(BF16), wk_ref[...], preferred_element_type=F32)
    for h in range(MLA_HEADS):
        k_ref[h, :, 0:LANES] = kn[:, h * LANES:(h + 1) * LANES].astype(BF16)
        k_ref[h, :, LANES:HEAD_PAD] = kr

    ca_t = ca.T
    sa_t = sa.T
    scale = (QK_NOPE + QK_ROPE) ** -0.5 * math.log2(math.e)
    qt = jnp.dot(wqt_ref[...], cq.T.astype(BF16), preferred_element_type=F32)
    for h in range(MLA_HEADS):
        base = h * HEAD_PAD
        qt_ref[h, 0, 0:LANES, :] = (qt[base:base + LANES] * scale).astype(BF16)
        r_in = qt[base + LANES:base + HEAD_PAD]
        r_sw = jnp.concatenate([r_in[half:], r_in[:half]], axis=0)
        qt_ref[h, 0, LANES:HEAD_PAD, :] = ((r_in * ca_t + r_sw * sa_t) * scale).astype(BF16)
    vt = jnp.dot(wvt_ref[...], ckv.T.astype(BF16), preferred_element_type=F32)
    ones_grp = jnp.where(lax.broadcasted_iota(jnp.int32, (V_EXT - V_HEAD, vt.shape[1]), 0) == 0,
                         1.0, 0.0).astype(BF16)
    for h in range(MLA_HEADS):
        vt_ref[h, 0, 0:V_HEAD, :] = vt[h * V_HEAD:(h + 1) * V_HEAD].astype(BF16)
        vt_ref[h, 0, V_HEAD:V_EXT, :] = ones_grp


def _mla_proj(x, pos, g_mix, w_a, g_q, g_kv, w_uq, w_ukv):
    S = x.shape[0]
    H = MLA_HEADS
    ts = PROJ_TS
    kr = w_a[:, Q_LORA + KV_LORA:]
    kr_sw = jnp.concatenate([kr[:, QK_ROPE // 2:], kr[:, :QK_ROPE // 2]], axis=1)
    wa_ext = jnp.concatenate([w_a, kr_sw], axis=1).astype(BF16)
    wq = w_uq.reshape(Q_LORA, H, QK_NOPE + QK_ROPE)
    wq_r = wq[:, :, QK_NOPE:]
    wq_sw = jnp.concatenate([wq_r[:, :, QK_ROPE // 2:], wq_r[:, :, :QK_ROPE // 2]], axis=2)
    wq_t = jnp.concatenate([wq, wq_sw], axis=2).reshape(Q_LORA, H * HEAD_PAD).T.astype(BF16)
    wkv = w_ukv.reshape(KV_LORA, H, QK_NOPE + V_HEAD)
    wk = wkv[:, :, :QK_NOPE].reshape(KV_LORA, H * QK_NOPE).astype(BF16)
    wv_t = wkv[:, :, QK_NOPE:].reshape(KV_LORA, H * V_HEAD).T.astype(BF16)
    inv = 1.0 / (ROPE_THETA ** (jnp.arange(0, QK_ROPE, 2, dtype=F32) / QK_ROPE))
    inv_row = jnp.tile(inv, LANES // inv.shape[0])[None, :]

    n = S // ts
    per_q = ATT_TQ // ts
    per_kv = ATT_TK // ts
    row = lambda i: (i, 0)
    return pl.pallas_call(
        _mla_proj_kernel,
        grid=(n,),
        in_specs=[
            pl.BlockSpec((ts, D_MODEL), row),
            pl.BlockSpec((ts, 1), row),
            _const_spec((1, D_MODEL)),
            _const_spec(wa_ext.shape),
            _const_spec((1, Q_LORA)),
            _const_spec((1, KV_LORA)),
            _const_spec(wq_t.shape),
            _const_spec(wk.shape),
            _const_spec(wv_t.shape),
            _const_spec((1, LANES)),
        ],
        out_specs=[
            pl.BlockSpec((H, 1, HEAD_PAD, ts), lambda i: (0, i // per_q, 0, i % per_q)),
            pl.BlockSpec((H, ts, HEAD_PAD), lambda i: (0, i, 0)),
            pl.BlockSpec((H, 1, V_EXT, ts), lambda i: (0, i // per_kv, 0, i % per_kv)),
        ],
        out_shape=[
            jax.ShapeDtypeStruct((H, S // ATT_TQ, HEAD_PAD, ATT_TQ), BF16),
            jax.ShapeDtypeStruct((H, S, HEAD_PAD), BF16),
            jax.ShapeDtypeStruct((H, S // ATT_TK, V_EXT, ATT_TK), BF16),
        ],
        compiler_params=_params("parallel"),
        name="mla_proj",
    )(x, pos, g_mix[None, :], wa_ext, g_q[None, :], g_kv[None, :], wq_t, wk, wv_t, inv_row)


def _attn_kernel(qt_ref, k_ref, vt_ref, o_ref, sa_ref, sb_ref, m_ref, acc_ref):
    i = pl.program_id(1)
    m_ref[...] = jnp.full(m_ref.shape, NEG_BIG, F32)
    acc_ref[...] = jnp.zeros(acc_ref.shape, F32)
    groups = ATT_TK // SUBLANES

    def scores(j, dst):
        ks = pl.multiple_of(j * ATT_TK, ATT_TK)
        dst[...] = jnp.dot(k_ref[0, pl.ds(ks, ATT_TK), :], qt_ref[0, 0],
                           preferred_element_type=F32)

    def softmax_pv(src, j, masked):
        s = src[...]
        if masked:
            ck = lax.broadcasted_iota(jnp.int32, s.shape, 0) // CHUNK
            rq = (q_off + lax.broadcasted_iota(jnp.int32, s.shape, 1)) // CHUNK
            s = jnp.where(ck <= rq, s, NEG_BIG)
        m_prev = m_ref[...]
        m8 = jnp.max(s.reshape(groups, SUBLANES, ATT_TQ), axis=0)
        m_new = jnp.maximum(m_prev, jnp.max(m8, axis=0, keepdims=True))
        alpha = jnp.exp2(m_prev - m_new)
        p = jnp.exp2(s - m_new)
        pv = jnp.dot(vt_ref[0, j], p.astype(BF16), preferred_element_type=F32)
        acc_ref[...] = acc_ref[...] * alpha + pv
        m_ref[...] = m_new

    kv_per_q = ATT_TK // ATT_TQ
    n_full = i // kv_per_q
    q_off = (i % kv_per_q) * ATT_TQ

    scores(0, sa_ref)

    def pair(t, carry):
        j = 2 * t
        scores(j + 1, sb_ref)
        softmax_pv(sa_ref, j, False)
        scores(j + 2, sa_ref)
        softmax_pv(sb_ref, j + 1, False)
        return carry

    lax.fori_loop(0, n_full // 2, pair, 0)

    @pl.when(n_full % 2 == 0)
    def _():
        softmax_pv(sa_ref, n_full, True)

    @pl.when(n_full % 2 == 1)
    def _():
        scores(n_full, sb_ref)
        softmax_pv(sa_ref, n_full - 1, False)
        softmax_pv(sb_ref, n_full, True)

    acc = acc_ref[...]
    o_ref[...] = (acc[:V_HEAD] / acc[V_HEAD:V_HEAD + 1]).T.astype(o_ref.dtype)


def _attention(qt, k, vt):
    H, S, _ = k.shape
    nq = S // ATT_TQ
    nk = S // ATT_TK
    return pl.pallas_call(
        _attn_kernel,
        grid=(H, nq),
        in_specs=[
            pl.BlockSpec((1, 1, HEAD_PAD, ATT_TQ), lambda h, i: (h, i, 0, 0)),
            pl.BlockSpec((1, S, HEAD_PAD), lambda h, i: (h, 0, 0)),
            pl.BlockSpec((1, nk, V_EXT, ATT_TK), lambda h, i: (h, 0, 0, 0)),
        ],
        out_specs=pl.BlockSpec((ATT_TQ, V_HEAD), lambda h, i: (i, h)),
        out_shape=jax.ShapeDtypeStruct((S, H * V_HEAD), BF16),
        scratch_shapes=[
            pltpu.VMEM((ATT_TK, ATT_TQ), F32),
            pltpu.VMEM((ATT_TK, ATT_TQ), F32),
            pltpu.VMEM((1, ATT_TQ), F32),
            pltpu.VMEM((V_EXT, ATT_TQ), F32),
        ],
        compiler_params=_params("parallel", "arbitrary"),
        name="mla_attention",
    )(qt, k, vt)


def _ffn_act(h, rows, g_ref, wup_ref, cw_ref, cb_ref, act_ref, tail_ref):
    ts = h.shape[0]
    hn = _rms(h, g_ref[...]).astype(BF16)

    def conv_up(col):
        cols = slice(col, col + FFN_CF)
        up = jnp.dot(hn, wup_ref[:, cols], preferred_element_type=F32)
        ext = jnp.concatenate([tail_ref[:, cols], up], axis=0)
        tail_ref[:, cols] = up[ts - FFN_TAIL:, :]
        w = cw_ref[:, cols]
        return (ext[FFN_TAIL - 2:FFN_TAIL - 2 + ts] * w[0:1]
                + ext[FFN_TAIL - 1:FFN_TAIL - 1 + ts] * w[1:2]
                + up * w[2:3] + cb_ref[:, cols])

    for c in range(D_FF // FFN_CF):
        val = conv_up(c * FFN_CF)
        gate = conv_up(D_FF + c * FFN_CF)
        act_ref[rows, c * FFN_CF:(c + 1) * FFN_CF] = (
            gate * jax.nn.sigmoid(gate) * val).astype(BF16)


def _ffn_groups(i, tail_ref, pre, post, g_ref, wup_ref, cw_ref, cb_ref, wdn_ref, act_ref):
    @pl.when(i == 0)
    def _():
        tail_ref[...] = jnp.zeros(tail_ref.shape, F32)

    groups = [slice(r0, r0 + FFN_SUB) for r0 in range(0, FFN_TS, FFN_SUB)]
    hs = [pre(rows) for rows in groups]
    pending = None
    for rows, h in zip(groups, hs):
        _ffn_act(h, rows, g_ref, wup_ref, cw_ref, cb_ref, act_ref, tail_ref)
        if pending is not None:
            post(*pending)
        down = jnp.dot(act_ref[rows, :], wdn_ref[...], preferred_element_type=F32)
        pending = (rows, h + down)
    post(*pending)


def _ffn0_kernel(x_ref, o_ref, wo_ref, g_ref, wup_ref, cw_ref, cb_ref, wdn_ref, gmix_ref,
                 win_ref, h_out_ref, u_out_ref, act_ref, tail_ref):
    def pre(rows):
        return x_ref[rows, :] + jnp.dot(o_ref[rows, :], wo_ref[...], preferred_element_type=F32)

    def post(rows, h2):
        h_out_ref[rows, :] = h2
        u_out_ref[rows, :] = jnp.dot(_rms(h2, gmix_ref[...]).astype(BF16), win_ref[...],
                                     preferred_element_type=F32)

    _ffn_groups(pl.program_id(0), tail_ref, pre, post, g_ref, wup_ref, cw_ref, cb_ref, wdn_ref,
                act_ref)


def _ffn1_kernel(h_ref, y_ref, u_ref, d_ref, wglu_ref, g_ref, wup_ref, cw_ref, cb_ref,
                 wdn_ref, gfin_ref, out_ref, act_ref, tail_ref):
    def pre(rows):
        yy = jax.nn.gelu(y_ref[rows, :] + d_ref[...] * u_ref[rows, :])
        z = jnp.dot(yy.astype(BF16), wglu_ref[...], preferred_element_type=F32)
        return h_ref[rows, :] + z[:, :D_MODEL] * jax.nn.sigmoid(z[:, D_MODEL:])

    def post(rows, out):
        out_ref[rows, :] = _rms(out, gfin_ref[...])

    _ffn_groups(pl.program_id(0), tail_ref, pre, post, g_ref, wup_ref, cw_ref, cb_ref, wdn_ref,
                act_ref)


def _layer_spec(stacked, layer):
    return pl.BlockSpec((None,) + stacked.shape[1:], lambda *_: (layer, 0, 0),
                        pipeline_mode=pl.Buffered(1))


def _ffn_specs(w_up_all, conv_w, w_down_all, layer):
    return [_const_spec((1, D_MODEL)), _layer_spec(w_up_all, layer), _const_spec(conv_w.shape),
            _const_spec((1, 2 * D_FF)), _layer_spec(w_down_all, layer)]


def _ffn_scratch():
    return [pltpu.VMEM((FFN_TS, D_FF), BF16), pltpu.VMEM((FFN_TAIL, 2 * D_FF), F32)]


def _attn_out_ffn_s5_in(x, o, w_o, g_ffn, w_up_all, conv_w, conv_b, w_down_all, g_mix, w_in):
    S = x.shape[0]
    row = lambda i: (i, 0)
    tile = pl.BlockSpec((FFN_TS, D_MODEL), row)
    return pl.pallas_call(
        _ffn0_kernel,
        grid=(S // FFN_TS,),
        in_specs=[tile, tile, _const_spec(w_o.shape)] + _ffn_specs(w_up_all, conv_w, w_down_all, 0)
                 + [_const_spec((1, D_MODEL)), _const_spec(w_in.shape)],
        out_specs=[tile, tile],
        out_shape=[jax.ShapeDtypeStruct((S, D_MODEL), F32)] * 2,
        scratch_shapes=_ffn_scratch(),
        compiler_params=_params("arbitrary"),
        name="conv_ffn0",
    )(x, o, w_o.astype(BF16), g_ffn[None, :], w_up_all, conv_w, conv_b[None, :],
      w_down_all, g_mix[None, :], w_in.astype(BF16))


def _s5_glu_ffn_final(h, y, u, d, w_glu, g_ffn, w_up_all, conv_w, conv_b, w_down_all, g_final):
    S = h.shape[0]
    row = lambda i: (i, 0)
    tile = pl.BlockSpec((FFN_TS, D_MODEL), row)
    return pl.pallas_call(
        _ffn1_kernel,
        grid=(S // FFN_TS,),
        in_specs=[tile, tile, tile, _const_spec((1, D_MODEL)), _const_spec(w_glu.shape)]
                 + _ffn_specs(w_up_all, conv_w, w_down_all, 1) + [_const_spec((1, D_MODEL))],
        out_specs=tile,
        out_shape=jax.ShapeDtypeStruct((S, D_MODEL), F32),
        scratch_shapes=_ffn_scratch(),
        compiler_params=_params("arbitrary"),
        name="conv_ffn1",
    )(h, y, u, d[None, :], w_glu.astype(BF16), g_ffn[None, :], w_up_all, conv_w,
      conv_b[None, :], w_down_all, g_final[None, :])


def _s5_kernel(ulo_ref, uhi_ref, bb_ref, cc_ref, a_ref, lp_ref, w_ref, y_ref,
               up_ref, buf_ref, xb_ref, yp_ref, carry_ref):
    t = pl.program_id(1)
    nc = S5_ST // LANES
    pair = 2 * S5_ROWS

    @pl.when(t == 0)
    def _():
        carry_ref[...] = jnp.zeros(carry_ref.shape, F32)

    per_sub = S5_L // S5_ROWS
    ch_rows = S5_CH * S5_ROWS
    grp_per_ch = S5_CH // S5_ROWS
    halves = S5_BLK // LANES

    def chunk_groups(k):
        for m in range(k * grp_per_ch, (k + 1) * grp_per_ch):
            for r in range(S5_ROWS):
                yield r * per_sub + m, m * S5_ROWS * S5_ROWS + r

    def col(ref_row, c):
        return jnp.broadcast_to(ref_row[:, c * LANES:(c + 1) * LANES], (S5_ROWS, LANES))

    ar = [col(a_ref[0, 0:1, :], c) for c in range(nc)]
    ai = [col(a_ref[0, 1:2, :], c) for c in range(nc)]

    def rows(l):
        return slice(l * S5_ROWS, (l + 1) * S5_ROWS)

    def re_cols(c):
        return slice(c * LANES, (c + 1) * LANES)

    def im_cols(c):
        return slice(S5_ST + c * LANES, S5_ST + (c + 1) * LANES)

    x = [jnp.zeros((S5_ROWS, LANES), F32)] * (2 * nc)
    for k in range(S5_L // S5_CH):
        for q, start in chunk_groups(k):
            for h, src in enumerate((ulo_ref, uhi_ref)):
                up_ref[h, pl.ds(start, S5_ROWS, stride=S5_ROWS), :] = (
                    src[q * S5_ROWS:(q + 1) * S5_ROWS, :])
        ck = slice(k * ch_rows, (k + 1) * ch_rows)
        up = jnp.concatenate([up_ref[h, ck, :] for h in range(halves)], axis=1).astype(BF16)
        buf_ref[ck, :] = jnp.dot(up, bb_ref[0], preferred_element_type=F32)
        for l in range(k * S5_CH, (k + 1) * S5_CH):
            for c in range(nc):
                xr, xi = x[2 * c], x[2 * c + 1]
                nr = ar[c] * xr - ai[c] * xi + buf_ref[rows(l), re_cols(c)]
                ni = ar[c] * xi + ai[c] * xr + buf_ref[rows(l), im_cols(c)]
                buf_ref[rows(l), re_cols(c)] = nr
                buf_ref[rows(l), im_cols(c)] = ni
                x[2 * c], x[2 * c + 1] = nr, ni
    ends = x

    sub = lax.broadcasted_iota(jnp.int32, (S5_ROWS, LANES), 0)
    prev = []
    for c in range(nc):
        sl = slice(c * LANES, (c + 1) * LANES)
        er, ei = ends[2 * c], ends[2 * c + 1]
        d = 1
        k = 0
        while d < S5_ROWS:
            pr = lp_ref[0, k, 0:1, sl]
            pi = lp_ref[0, k, 1:2, sl]
            sr = jnp.where(sub >= d, pltpu.roll(er, d, 0), 0.0)
            si = jnp.where(sub >= d, pltpu.roll(ei, d, 0), 0.0)
            er, ei = er + pr * sr - pi * si, ei + pr * si + pi * sr
            d *= 2
            k += 1
        cr = carry_ref[0:1, sl]
        ci = carry_ref[1:2, sl]
        wr = w_ref[0, 0, :, sl]
        wi = w_ref[0, 1, :, sl]
        er, ei = er + wr * cr - wi * ci, ei + wr * ci + wi * cr
        carry_ref[0:1, sl] = er[S5_ROWS - 1:S5_ROWS, :]
        carry_ref[1:2, sl] = ei[S5_ROWS - 1:S5_ROWS, :]
        prev += [jnp.where(sub >= 1, pltpu.roll(er, 1, 0), jnp.broadcast_to(cr, (S5_ROWS, LANES))),
                 jnp.where(sub >= 1, pltpu.roll(ei, 1, 0), jnp.broadcast_to(ci, (S5_ROWS, LANES)))]

    p = prev
    for k in range(S5_L // S5_CH):
        for j in range(k * S5_CH // 2, (k + 1) * S5_CH // 2):
            dst = slice(j * pair, (j + 1) * pair)
            for c in range(nc):
                p0r = ar[c] * p[2 * c] - ai[c] * p[2 * c + 1]
                p0i = ar[c] * p[2 * c + 1] + ai[c] * p[2 * c]
                p1r = ar[c] * p0r - ai[c] * p0i
                p1i = ar[c] * p0i + ai[c] * p0r
                xr = jnp.concatenate([buf_ref[rows(2 * j), re_cols(c)] + p0r,
                                      buf_ref[rows(2 * j + 1), re_cols(c)] + p1r], axis=0)
                xi = jnp.concatenate([buf_ref[rows(2 * j), im_cols(c)] + p0i,
                                      buf_ref[rows(2 * j + 1), im_cols(c)] + p1i], axis=0)
                xb_ref[dst, re_cols(c)] = xr.astype(BF16)
                xb_ref[dst, im_cols(c)] = xi.astype(BF16)
                p[2 * c], p[2 * c + 1] = p1r, p1i
        ck = slice(k * ch_rows, (k + 1) * ch_rows)
        yp = jnp.dot(xb_ref[ck, :], cc_ref[0], preferred_element_type=F32)
        for h in range(halves):
            yp_ref[h, ck, :] = yp[:, h * LANES:(h + 1) * LANES]
        for q, start in chunk_groups(k):
            for h in range(halves):
                y_ref[q * S5_ROWS:(q + 1) * S5_ROWS, h * LANES:(h + 1) * LANES] = (
                    yp_ref[h, pl.ds(start, S5_ROWS, stride=S5_ROWS), :])


def _s5_tables(lam_re, lam_im, log_dt, b_re, b_im, c_re, c_im):
    G, P, C = SSM_GROUPS, SSM_STATE, SSM_GROUP
    nb, gb = S5_NBLK, SSM_GROUPS // S5_NBLK
    dt = jnp.exp(log_dt.astype(F32))[:, None]
    lr = lam_re.astype(F32)
    li = lam_im.astype(F32)
    mag = jnp.exp(lr * dt)
    ar = mag * jnp.cos(li * dt)
    ai = mag * jnp.sin(li * dt)
    den = lr * lr + li * li
    nr = ar - 1.0
    coef_r = (nr * lr + ai * li) / den
    coef_i = (ai * lr - nr * li) / den
    br = b_re.astype(F32)
    bi = b_im.astype(F32)
    bbar_r = coef_r[..., None] * br - coef_i[..., None] * bi
    bbar_i = coef_r[..., None] * bi + coef_i[..., None] * br

    def blockdiag(m, rows_per_group, cols_per_group):
        m = m.reshape(nb, gb * rows_per_group, cols_per_group)
        tiled = jnp.tile(m, (1, 1, gb))
        row_g = lax.broadcasted_iota(jnp.int32, tiled.shape, 1) // rows_per_group
        col_g = lax.broadcasted_iota(jnp.int32, tiled.shape, 2) // cols_per_group
        return jnp.where(row_g == col_g, tiled, 0.0)

    def blockdiag_in(m):
        return blockdiag(jnp.swapaxes(m, 1, 2), C, P)

    def blockdiag_out(m):
        return blockdiag(jnp.swapaxes(m, 1, 2), P, C)

    bb = jnp.concatenate([blockdiag_in(bbar_r), blockdiag_in(bbar_i)], axis=2).astype(BF16)
    cc = jnp.concatenate([blockdiag_out(c_re.astype(F32)),
                          -blockdiag_out(c_im.astype(F32))], axis=1).astype(BF16)

    def apow(nsteps):
        m = jnp.exp(lr * dt * nsteps)
        return m * jnp.cos(li * dt * nsteps), m * jnp.sin(li * dt * nsteps)

    def blk(x):
        lead = x.shape[:-2]
        x = x.reshape(lead + (nb, gb * P))
        return jnp.moveaxis(x, -2, 0)

    a_tab = jnp.stack([blk(ar), blk(ai)], axis=1)
    lp = []
    d = 1
    while d < S5_ROWS:
        pr, pi = apow(float(S5_L * d))
        lp.append(jnp.stack([blk(pr), blk(pi)], axis=1))
        d *= 2
    lp_tab = jnp.stack(lp, axis=1)
    steps = (S5_L * (jnp.arange(S5_ROWS, dtype=F32) + 1.0))[:, None, None]
    wr, wi = apow(steps)
    w_tab = jnp.stack([blk(wr), blk(wi)], axis=1)
    return bb, cc, a_tab, lp_tab, w_tab


def _s5_core(u, bb, cc, a_tab, lp_tab, w_tab):
    S = u.shape[0]
    nt = S // S5_TT
    return pl.pallas_call(
        _s5_kernel,
        grid=(S5_NBLK, nt),
        in_specs=[
            pl.BlockSpec((S5_TT, LANES), lambda b, t: (t, 2 * b)),
            pl.BlockSpec((S5_TT, LANES), lambda b, t: (t, 2 * b + 1)),
            pl.BlockSpec((1, S5_BLK, 2 * S5_ST), lambda b, t: (b, 0, 0)),
            pl.BlockSpec((1, 2 * S5_ST, S5_BLK), lambda b, t: (b, 0, 0)),
            pl.BlockSpec((1, 2, S5_ST), lambda b, t: (b, 0, 0)),
            pl.BlockSpec((1,) + lp_tab.shape[1:], lambda b, t: (b, 0, 0, 0)),
            pl.BlockSpec((1, 2, S5_ROWS, S5_ST), lambda b, t: (b, 0, 0, 0)),
        ],
        out_specs=pl.BlockSpec((S5_TT, S5_BLK), lambda b, t: (t, b)),
        out_shape=jax.ShapeDtypeStruct((S, D_MODEL), F32),
        scratch_shapes=[pltpu.VMEM((S5_BLK // LANES, S5_TT, LANES), F32),
                        pltpu.VMEM((S5_TT, 2 * S5_ST), F32),
                        pltpu.VMEM((S5_TT, 2 * S5_ST), BF16),
                        pltpu.VMEM((S5_BLK // LANES, S5_TT, LANES), F32),
                        pltpu.VMEM((2, S5_ST), F32)],
        compiler_params=_params("parallel", "arbitrary"),
        name="s5_core",
    )(u, u, bb, cc, a_tab, lp_tab, w_tab)


def kernel(x, positions, mla_w_a, mla_g_q, mla_g_kv, mla_w_uq, mla_w_ukv, mla_w_o,
           ssm_w_in, ssm_lambda_re, ssm_lambda_im, ssm_log_dt, ssm_b_re, ssm_b_im,
           ssm_c_re, ssm_c_im, ssm_d, ssm_w_glu, ffn_w_up, ffn_conv_w, ffn_conv_b,
           ffn_w_down, g_mix, g_ffn, g_final):
    B, S, D = x.shape
    assert (B, S, D) == (1, SEQ, D_MODEL)
    h = x.reshape(S, D)
    pos = positions.reshape(S, 1)

    qt, k, vt = _mla_proj(h, pos, g_mix[0], mla_w_a[0], mla_g_q[0], mla_g_kv[0],
                         mla_w_uq[0], mla_w_ukv[0])
    o = _attention(qt, k, vt)
    w_up_all = ffn_w_up.astype(BF16)
    w_down_all = ffn_w_down.astype(BF16)
    h, u = _attn_out_ffn_s5_in(h, o, mla_w_o[0], g_ffn[0], w_up_all, ffn_conv_w[0],
                               ffn_conv_b[0], w_down_all, g_mix[1], ssm_w_in[0])

    tabs = _s5_tables(ssm_lambda_re[0], ssm_lambda_im[0], ssm_log_dt[0],
                      ssm_b_re[0], ssm_b_im[0], ssm_c_re[0], ssm_c_im[0])
    y = _s5_core(u, *tabs)
    h = _s5_glu_ffn_final(h, y, u, ssm_d[0], ssm_w_glu[0], g_ffn[1], w_up_all,
                          ffn_conv_w[1], ffn_conv_b[1], w_down_all, g_final)
    return h.reshape(B, S, D)
```

```python
import functools
import math

import jax
import jax.numpy as jnp
from jax import lax
from jax.experimental import pallas as pl
from jax.experimental.pallas import tpu as pltpu

F32 = jnp.float32
BF16 = jnp.bfloat16

D_MODEL = 1024
SEQ = 16384
EPS = 1e-6
CHUNK = 64
MLA_HEADS = 8
QK_NOPE = 128
QK_ROPE = 64
V_HEAD = 128
Q_LORA = 384
KV_LORA = 256
ROPE_THETA = 10000.0
SSM_GROUP = 16
SSM_GROUPS = 64
SSM_STATE = 64
D_FF = 2816
CONV_W = 3

LANES = 128
SUBLANES = 8
HEAD_PAD = 2 * LANES
BF16_ROWS = 16
VMEM_LIMIT = 56 * 1024 * 1024

PROJ_TS = 512
ATT_TQ = 1024
ATT_TK = 1024
FFN_TS = 512
FFN_TAIL = SUBLANES
FFN_CF = 256
S5_TT = 1024
S5_ROWS = SUBLANES
S5_L = S5_TT // S5_ROWS
S5_CH = 16
S5_BLK = 2 * LANES
S5_NBLK = D_MODEL // S5_BLK
S5_ST = S5_BLK // SSM_GROUP * SSM_STATE
V_EXT = V_HEAD + BF16_ROWS
NEG_BIG = -1e30


def _rms(x, g):
    return x * lax.rsqrt(jnp.mean(x * x, axis=-1, keepdims=True) + EPS) * g


def _const_spec(shape):
    nd = len(shape)
    return pl.BlockSpec(shape, lambda *_: (0,) * nd, pipeline_mode=pl.Buffered(1))


def _params(*sem):
    return pltpu.CompilerParams(dimension_semantics=sem, vmem_limit_bytes=VMEM_LIMIT)


def _mla_proj_kernel(x_ref, pos_ref, gmix_ref, wa_ref, gq_ref, gkv_ref, wqt_ref,
                     wk_ref, wvt_ref, inv_ref, qt_ref, k_ref, vt_ref):
    hn = _rms(x_ref[...], gmix_ref[...])
    a = jnp.dot(hn.astype(BF16), wa_ref[...], preferred_element_type=F32)
    cq = _rms(a[:, :Q_LORA], gq_ref[...])
    ckv = _rms(a[:, Q_LORA:Q_LORA + KV_LORA], gkv_ref[...])

    half = LANES // 2
    nfreq = QK_ROPE // 2
    pack = LANES // nfreq
    blk = x_ref.shape[0] // pack
    lane = lax.broadcasted_iota(jnp.int32, (blk, LANES), 1)
    pos_p = pos_ref[(pack - 1) * blk:pack * blk, :]
    for b in range(pack - 2, -1, -1):
        pos_p = jnp.where(lane < (b + 1) * nfreq, pos_ref[b * blk:(b + 1) * blk, :], pos_p)
    ang = pos_p.astype(F32) * inv_ref[...]
    cos_p = jnp.cos(ang)
    sin_p = jnp.sin(ang)
    ca_blocks, sa_blocks = [], []
    for b in range(pack):
        shift = (LANES - b * nfreq) % LANES
        c0 = pltpu.roll(cos_p, shift, 1) if shift else cos_p
        s0 = pltpu.roll(sin_p, shift, 1) if shift else sin_p
        c1 = pltpu.roll(c0, nfreq, 1)
        s1 = pltpu.roll(s0, nfreq, 1)
        ca_blocks.append(jnp.where(lane < nfreq, c0, jnp.where(lane < 2 * nfreq, c1, 0.0)))
        sa_blocks.append(jnp.where(lane < nfreq, -s0, jnp.where(lane < 2 * nfreq, s1, 0.0)))
    ca = jnp.concatenate(ca_blocks, axis=0)
    sa = jnp.concatenate(sa_blocks, axis=0)

    kr_in = a[:, Q_LORA + KV_LORA:]
    kr = (kr_in * ca + pltpu.roll(kr_in, half, 1) * sa).astype(BF16)
    kn = jnp.dot(ckv.astype(BF16), wk_ref[...], preferred_element_type=F32)
    for h in range(MLA_HEADS):
        k_ref[h, :, 0:LANES] = kn[:, h * LANES:(h + 1) * LANES].astype(BF16)
        k_ref[h, :, LANES:HEAD_PAD] = kr

    ca_t = ca.T
    sa_t = sa.T
    scale = (QK_NOPE + QK_ROPE) ** -0.5 * math.log2(math.e)
    qt = jnp.dot(wqt_ref[...], cq.T.astype(BF16), preferred_element_type=F32)
    for h in range(MLA_HEADS):
        base = h * HEAD_PAD
        qt_ref[h, 0, 0:LANES, :] = (qt[base:base + LANES] * scale).astype(BF16)
        r_in = qt[base + LANES:base + HEAD_PAD]
        r_sw = jnp.concatenate([r_in[half:], r_in[:half]], axis=0)
        qt_ref[h, 0, LANES:HEAD_PAD, :] = ((r_in * ca_t + r_sw * sa_t) * scale).astype(BF16)
    vt = jnp.dot(wvt_ref[...], ckv.T.astype(BF16), preferred_element_type=F32)
    ones_grp = jnp.where(lax.broadcasted_iota(jnp.int32, (V_EXT - V_HEAD, vt.shape[1]), 0) == 0,
                         1.0, 0.0).astype(BF16)
    for h in range(MLA_HEADS):
        vt_ref[h, 0, 0:V_HEAD, :] = vt[h * V_HEAD:(h + 1) * V_HEAD].astype(BF16)
        vt_ref[h, 0, V_HEAD:V_EXT, :] = ones_grp


def _mla_proj(x, pos, g_mix, w_a, g_q, g_kv, w_uq, w_ukv):
    S = x.shape[0]
    H = MLA_HEADS
    ts = PROJ_TS
    kr = w_a[:, Q_LORA + KV_LORA:]
    kr_sw = jnp.concatenate([kr[:, QK_ROPE // 2:], kr[:, :QK_ROPE // 2]], axis=1)
    wa_ext = jnp.concatenate([w_a, kr_sw], axis=1).astype(BF16)
    wq = w_uq.reshape(Q_LORA, H, QK_NOPE + QK_ROPE)
    wq_r = wq[:, :, QK_NOPE:]
    wq_sw = jnp.concatenate([wq_r[:, :, QK_ROPE // 2:], wq_r[:, :, :QK_ROPE // 2]], axis=2)
    wq_t = jnp.concatenate([wq, wq_sw], axis=2).reshape(Q_LORA, H * HEAD_PAD).T.astype(BF16)
    wkv = w_ukv.reshape(KV_LORA, H, QK_NOPE + V_HEAD)
    wk = wkv[:, :, :QK_NOPE].reshape(KV_LORA, H * QK_NOPE).astype(BF16)
    wv_t = wkv[:, :, QK_NOPE:].reshape(KV_LORA, H * V_HEAD).T.astype(BF16)
    inv = 1.0 / (ROPE_THETA ** (jnp.arange(0, QK_ROPE, 2, dtype=F32) / QK_ROPE))
    inv_row = jnp.tile(inv, LANES // inv.shape[0])[None, :]

    n = S // ts
    per_q = ATT_TQ // ts
    per_kv = ATT_TK // ts
    row = lambda i: (i, 0)
    return pl.pallas_call(
        _mla_proj_kernel,
        grid=(n,),
        in_specs=[
            pl.BlockSpec((ts, D_MODEL), row),
            pl.BlockSpec((ts, 1), row),
            _const_spec((1, D_MODEL)),
            _const_spec(wa_ext.shape),
            _const_spec((1, Q_LORA)),
            _const_spec((1, KV_LORA)),
            _const_spec(wq_t.shape),
            _const_spec(wk.shape),
            _const_spec(wv_t.shape),
            _const_spec((1, LANES)),
        ],
        out_specs=[
            pl.BlockSpec((H, 1, HEAD_PAD, ts), lambda i: (0, i // per_q, 0, i % per_q)),
            pl.BlockSpec((H, ts, HEAD_PAD), lambda i: (0, i, 0)),
            pl.BlockSpec((H, 1, V_EXT, ts), lambda i: (0, i // per_kv, 0, i % per_kv)),
        ],
        out_shape=[
            jax.ShapeDtypeStruct((H, S // ATT_TQ, HEAD_PAD, ATT_TQ), BF16),
            jax.ShapeDtypeStruct((H, S, HEAD_PAD), BF16),
            jax.ShapeDtypeStruct((H, S // ATT_TK, V_EXT, ATT_TK), BF16),
        ],
        compiler_params=_params("parallel"),
        name="mla_proj",
    )(x, pos, g_mix[None, :], wa_ext, g_q[None, :], g_kv[None, :], wq_t, wk, wv_t, inv_row)


def _attn_kernel(qt_ref, k_ref, vt_ref, o_ref, sa_ref, sb_ref, m_ref, acc_ref):
    i = pl.program_id(1)
    m_ref[...] = jnp.full(m_ref.shape, NEG_BIG, F32)
    acc_ref[...] = jnp.zeros(acc_ref.shape, F32)
    groups = ATT_TK // SUBLANES

    def scores(j, dst):
        ks = pl.multiple_of(j * ATT_TK, ATT_TK)
        dst[...] = jnp.dot(k_ref[0, pl.ds(ks, ATT_TK), :], qt_ref[0, 0],
                           preferred_element_type=F32)

    def softmax_pv(src, j, masked):
        s = src[...]
        if masked:
            ck = lax.broadcasted_iota(jnp.int32, s.shape, 0) // CHUNK
            rq = (q_off + lax.broadcasted_iota(jnp.int32, s.shape, 1)) // CHUNK
            s = jnp.where(ck <= rq, s, NEG_BIG)
        m_prev = m_ref[...]
        m8 = jnp.max(s.reshape(groups, SUBLANES, ATT_TQ), axis=0)
        m_new = jnp.maximum(m_prev, jnp.max(m8, axis=0, keepdims=True))
        alpha = jnp.exp2(m_prev - m_new)
        p = jnp.exp2(s - m_new)
        pv = jnp.dot(vt_ref[0, j], p.astype(BF16), preferred_element_type=F32)
        acc_ref[...] = acc_ref[...] * alpha + pv
        m_ref[...] = m_new

    kv_per_q = ATT_TK // ATT_TQ
    n_full = i // kv_per_q
    q_off = (i % kv_per_q) * ATT_TQ

    scores(0, sa_ref)

    def pair(t, carry):
        j = 2 * t
        scores(j + 1, sb_ref)
        softmax_pv(sa_ref, j, False)
        scores(j + 2, sa_ref)
        softmax_pv(sb_ref, j + 1, False)
        return carry

    lax.fori_loop(0, n_full // 2, pair, 0)

    @pl.when(n_full % 2 == 0)
    def _():
        softmax_pv(sa_ref, n_full, True)

    @pl.when(n_full % 2 == 1)
    def _():
        scores(n_full, sb_ref)
        softmax_pv(sa_ref, n_full - 1, False)
        softmax_pv(sb_ref, n_full, True)

    acc = acc_ref[...]
    o_ref[...] = (acc[:V_HEAD] / acc[V_HEAD:V_HEAD + 1]).T.astype(o_ref.dtype)


def _attention(qt, k, vt):
    H, S, _ = k.shape
    nq = S // ATT_TQ
    nk = S // ATT_TK
    return pl.pallas_call(
        _attn_kernel,
        grid=(H, nq),
        in_specs=[
            pl.BlockSpec((1, 1, HEAD_PAD, ATT_TQ), lambda h, i: (h, i, 0, 0)),
            pl.BlockSpec((1, S, HEAD_PAD), lambda h, i: (h, 0, 0)),
            pl.BlockSpec((1, nk, V_EXT, ATT_TK), lambda h, i: (h, 0, 0, 0)),
        ],
        out_specs=pl.BlockSpec((ATT_TQ, V_HEAD), lambda h, i: (i, h)),
        out_shape=jax.ShapeDtypeStruct((S, H * V_HEAD), BF16),
        scratch_shapes=[
            pltpu.VMEM((ATT_TK, ATT_TQ), F32),
            pltpu.VMEM((ATT_TK, ATT_TQ), F32),
            pltpu.VMEM((1, ATT_TQ), F32),
            pltpu.VMEM((V_EXT, ATT_TQ), F32),
        ],
        compiler_params=_params("parallel", "arbitrary"),
        name="mla_attention",
    )(qt, k, vt)


def _ffn_body(h, i, g_ref, wup_ref, cw_ref, cb_ref, wdn_ref, act_ref, tail_ref):
    ts = h.shape[0]
    hn = _rms(h, g_ref[...]).astype(BF16)

    @pl.when(i == 0)
    def _():
        tail_ref[...] = jnp.zeros(tail_ref.shape, F32)

    def conv_up(col):
        cols = slice(col, col + FFN_CF)
        up = jnp.dot(hn, wup_ref[:, cols], preferred_element_type=F32)
        ext = jnp.concatenate([tail_ref[:, cols], up], axis=0)
        tail_ref[:, cols] = up[ts - FFN_TAIL:, :]
        w = cw_ref[:, cols]
        return (ext[FFN_TAIL - 2:FFN_TAIL - 2 + ts] * w[0:1]
                + ext[FFN_TAIL - 1:FFN_TAIL - 1 + ts] * w[1:2]
                + up * w[2:3] + cb_ref[:, cols])

    for c in range(D_FF // FFN_CF):
        val = conv_up(c * FFN_CF)
        gate = conv_up(D_FF + c * FFN_CF)
        act_ref[:, c * FFN_CF:(c + 1) * FFN_CF] = (gate * jax.nn.sigmoid(gate) * val).astype(BF16)

    return h + jnp.dot(act_ref[...], wdn_ref[...], preferred_element_type=F32)


def _ffn0_kernel(x_ref, o_ref, wo_ref, g_ref, wup_ref, cw_ref, cb_ref, wdn_ref, gmix_ref,
                 win_ref, h_out_ref, u_out_ref, act_ref, tail_ref):
    i = pl.program_id(0)
    h1 = x_ref[...] + jnp.dot(o_ref[...], wo_ref[...], preferred_element_type=F32)
    h2 = _ffn_body(h1, i, g_ref, wup_ref, cw_ref, cb_ref, wdn_ref, act_ref, tail_ref)
    h_out_ref[...] = h2
    u_out_ref[...] = jnp.dot(_rms(h2, gmix_ref[...]).astype(BF16), win_ref[...],
                             preferred_element_type=F32)


def _ffn1_kernel(h_ref, y_ref, u_ref, d_ref, wglu_ref, g_ref, wup_ref, cw_ref, cb_ref,
                 wdn_ref, gfin_ref, out_ref, act_ref, tail_ref):
    i = pl.program_id(0)
    yy = jax.nn.gelu(y_ref[...] + d_ref[...] * u_ref[...])
    z = jnp.dot(yy.astype(BF16), wglu_ref[...], preferred_element_type=F32)
    h3 = h_ref[...] + z[:, :D_MODEL] * jax.nn.sigmoid(z[:, D_MODEL:])
    out = _ffn_body(h3, i, g_ref, wup_ref, cw_ref, cb_ref, wdn_ref, act_ref, tail_ref)
    out_ref[...] = _rms(out, gfin_ref[...])


def _layer_spec(stacked, layer):
    return pl.BlockSpec((None,) + stacked.shape[1:], lambda *_: (layer, 0, 0),
                        pipeline_mode=pl.Buffered(1))


def _ffn_specs(w_up_all, conv_w, w_down_all, layer):
    return [_const_spec((1, D_MODEL)), _layer_spec(w_up_all, layer), _const_spec(conv_w.shape),
            _const_spec((1, 2 * D_FF)), _layer_spec(w_down_all, layer)]


def _ffn_scratch():
    return [pltpu.VMEM((FFN_TS, D_FF), BF16), pltpu.VMEM((FFN_TAIL, 2 * D_FF), F32)]


def _attn_out_ffn_s5_in(x, o, w_o, g_ffn, w_up_all, conv_w, conv_b, w_down_all, g_mix, w_in):
    S = x.shape[0]
    row = lambda i: (i, 0)
    tile = pl.BlockSpec((FFN_TS, D_MODEL), row)
    return pl.pallas_call(
        _ffn0_kernel,
        grid=(S // FFN_TS,),
        in_specs=[tile, tile, _const_spec(w_o.shape)] + _ffn_specs(w_up_all, conv_w, w_down_all, 0)
                 + [_const_spec((1, D_MODEL)), _const_spec(w_in.shape)],
        out_specs=[tile, tile],
        out_shape=[jax.ShapeDtypeStruct((S, D_MODEL), F32)] * 2,
        scratch_shapes=_ffn_scratch(),
        compiler_params=_params("arbitrary"),
        name="conv_ffn0",
    )(x, o, w_o.astype(BF16), g_ffn[None, :], w_up_all, conv_w, conv_b[None, :],
      w_down_all, g_mix[None, :], w_in.astype(BF16))


def _s5_glu_ffn_final(h, y, u, d, w_glu, g_ffn, w_up_all, conv_w, conv_b, w_down_all, g_final):
    S = h.shape[0]
    row = lambda i: (i, 0)
    tile = pl.BlockSpec((FFN_TS, D_MODEL), row)
    return pl.pallas_call(
        _ffn1_kernel,
        grid=(S // FFN_TS,),
        in_specs=[tile, tile, tile, _const_spec((1, D_MODEL)), _const_spec(w_glu.shape)]
                 + _ffn_specs(w_up_all, conv_w, w_down_all, 1) + [_const_spec((1, D_MODEL))],
        out_specs=tile,
        out_shape=jax.ShapeDtypeStruct((S, D_MODEL), F32),
        scratch_shapes=_ffn_scratch(),
        compiler_params=_params("arbitrary"),
        name="conv_ffn1",
    )(h, y, u, d[None, :], w_glu.astype(BF16), g_ffn[None, :], w_up_all, conv_w,
      conv_b[None, :], w_down_all, g_final[None, :])


def _s5_kernel(ulo_ref, uhi_ref, bb_ref, cc_ref, a_ref, lp_ref, w_ref, y_ref,
               up_ref, buf_ref, xb_ref, yp_ref, carry_ref):
    t = pl.program_id(1)
    nc = S5_ST // LANES
    pair = 2 * S5_ROWS

    @pl.when(t == 0)
    def _():
        carry_ref[...] = jnp.zeros(carry_ref.shape, F32)

    per_sub = S5_L // S5_ROWS
    ch_rows = S5_CH * S5_ROWS
    grp_per_ch = S5_CH // S5_ROWS
    halves = S5_BLK // LANES

    def chunk_groups(k):
        for m in range(k * grp_per_ch, (k + 1) * grp_per_ch):
            for r in range(S5_ROWS):
                yield r * per_sub + m, m * S5_ROWS * S5_ROWS + r

    def col(ref_row, c):
        return jnp.broadcast_to(ref_row[:, c * LANES:(c + 1) * LANES], (S5_ROWS, LANES))

    ar = [col(a_ref[0, 0:1, :], c) for c in range(nc)]
    ai = [col(a_ref[0, 1:2, :], c) for c in range(nc)]

    def rows(l):
        return slice(l * S5_ROWS, (l + 1) * S5_ROWS)

    def re_cols(c):
        return slice(c * LANES, (c + 1) * LANES)

    def im_cols(c):
        return slice(S5_ST + c * LANES, S5_ST + (c + 1) * LANES)

    x = [jnp.zeros((S5_ROWS, LANES), F32)] * (2 * nc)
    for k in range(S5_L // S5_CH):
        for q, start in chunk_groups(k):
            for h, src in enumerate((ulo_ref, uhi_ref)):
                up_ref[h, pl.ds(start, S5_ROWS, stride=S5_ROWS), :] = (
                    src[q * S5_ROWS:(q + 1) * S5_ROWS, :])
        ck = slice(k * ch_rows, (k + 1) * ch_rows)
        up = jnp.concatenate([up_ref[h, ck, :] for h in range(halves)], axis=1).astype(BF16)
        buf_ref[ck, :] = jnp.dot(up, bb_ref[0], preferred_element_type=F32)
        for l in range(k * S5_CH, (k + 1) * S5_CH):
            for c in range(nc):
                xr, xi = x[2 * c], x[2 * c + 1]
                nr = ar[c] * xr - ai[c] * xi + buf_ref[rows(l), re_cols(c)]
                ni = ar[c] * xi + ai[c] * xr + buf_ref[rows(l), im_cols(c)]
                buf_ref[rows(l), re_cols(c)] = nr
                buf_ref[rows(l), im_cols(c)] = ni
                x[2 * c], x[2 * c + 1] = nr, ni
    ends = x

    sub = lax.broadcasted_iota(jnp.int32, (S5_ROWS, LANES), 0)
    prev = []
    for c in range(nc):
        sl = slice(c * LANES, (c + 1) * LANES)
        er, ei = ends[2 * c], ends[2 * c + 1]
        d = 1
        k = 0
        while d < S5_ROWS:
            pr = lp_ref[0, k, 0:1, sl]
            pi = lp_ref[0, k, 1:2, sl]
            sr = jnp.where(sub >= d, pltpu.roll(er, d, 0), 0.0)
            si = jnp.where(sub >= d, pltpu.roll(ei, d, 0), 0.0)
            er, ei = er + pr * sr - pi * si, ei + pr * si + pi * sr
            d *= 2
            k += 1
        cr = carry_ref[0:1, sl]
        ci = carry_ref[1:2, sl]
        wr = w_ref[0, 0, :, sl]
        wi = w_ref[0, 1, :, sl]
        er, ei = er + wr * cr - wi * ci, ei + wr * ci + wi * cr
        carry_ref[0:1, sl] = er[S5_ROWS - 1:S5_ROWS, :]
        carry_ref[1:2, sl] = ei[S5_ROWS - 1:S5_ROWS, :]
        prev += [jnp.where(sub >= 1, pltpu.roll(er, 1, 0), jnp.broadcast_to(cr, (S5_ROWS, LANES))),
                 jnp.where(sub >= 1, pltpu.roll(ei, 1, 0), jnp.broadcast_to(ci, (S5_ROWS, LANES)))]

    p = prev
    for k in range(S5_L // S5_CH):
        for j in range(k * S5_CH // 2, (k + 1) * S5_CH // 2):
            dst = slice(j * pair, (j + 1) * pair)
            for c in range(nc):
                p0r = ar[c] * p[2 * c] - ai[c] * p[2 * c + 1]
                p0i = ar[c] * p[2 * c + 1] + ai[c] * p[2 * c]
                p1r = ar[c] * p0r - ai[c] * p0i
                p1i = ar[c] * p0i + ai[c] * p0r
                xr = jnp.concatenate([buf_ref[rows(2 * j), re_cols(c)] + p0r,
                                      buf_ref[rows(2 * j + 1), re_cols(c)] + p1r], axis=0)
                xi = jnp.concatenate([buf_ref[rows(2 * j), im_cols(c)] + p0i,
                                      buf_ref[rows(2 * j + 1), im_cols(c)] + p1i], axis=0)
                xb_ref[dst, re_cols(c)] = xr.astype(BF16)
                xb_ref[dst, im_cols(c)] = xi.astype(BF16)
                p[2 * c], p[2 * c + 1] = p1r, p1i
        ck = slice(k * ch_rows, (k + 1) * ch_rows)
        yp = jnp.dot(xb_ref[ck, :], cc_ref[0], preferred_element_type=F32)
        for h in range(halves):
            yp_ref[h, ck, :] = yp[:, h * LANES:(h + 1) * LANES]
        for q, start in chunk_groups(k):
            for h in range(halves):
                y_ref[q * S5_ROWS:(q + 1) * S5_ROWS, h * LANES:(h + 1) * LANES] = (
                    yp_ref[h, pl.ds(start, S5_ROWS, stride=S5_ROWS), :])


def _s5_tables(lam_re, lam_im, log_dt, b_re, b_im, c_re, c_im):
    G, P, C = SSM_GROUPS, SSM_STATE, SSM_GROUP
    nb, gb = S5_NBLK, SSM_GROUPS // S5_NBLK
    dt = jnp.exp(log_dt.astype(F32))[:, None]
    lr = lam_re.astype(F32)
    li = lam_im.astype(F32)
    mag = jnp.exp(lr * dt)
    ar = mag * jnp.cos(li * dt)
    ai = mag * jnp.sin(li * dt)
    den = lr * lr + li * li
    nr = ar - 1.0
    coef_r = (nr * lr + ai * li) / den
    coef_i = (ai * lr - nr * li) / den
    br = b_re.astype(F32)
    bi = b_im.astype(F32)
    bbar_r = coef_r[..., None] * br - coef_i[..., None] * bi
    bbar_i = coef_r[..., None] * bi + coef_i[..., None] * br

    def blockdiag(m, rows_per_group, cols_per_group):
        m = m.reshape(nb, gb * rows_per_group, cols_per_group)
        tiled = jnp.tile(m, (1, 1, gb))
        row_g = lax.broadcasted_iota(jnp.int32, tiled.shape, 1) // rows_per_group
        col_g = lax.broadcasted_iota(jnp.int32, tiled.shape, 2) // cols_per_group
        return jnp.where(row_g == col_g, tiled, 0.0)

    def blockdiag_in(m):
        return blockdiag(jnp.swapaxes(m, 1, 2), C, P)

    def blockdiag_out(m):
        return blockdiag(jnp.swapaxes(m, 1, 2), P, C)

    bb = jnp.concatenate([blockdiag_in(bbar_r), blockdiag_in(bbar_i)], axis=2).astype(BF16)
    cc = jnp.concatenate([blockdiag_out(c_re.astype(F32)),
                          -blockdiag_out(c_im.astype(F32))], axis=1).astype(BF16)

    def apow(nsteps):
        m = jnp.exp(lr * dt * nsteps)
        return m * jnp.cos(li * dt * nsteps), m * jnp.sin(li * dt * nsteps)

    def blk(x):
        lead = x.shape[:-2]
        x = x.reshape(lead + (nb, gb * P))
        return jnp.moveaxis(x, -2, 0)

    a_tab = jnp.stack([blk(ar), blk(ai)], axis=1)
    lp = []
    d = 1
    while d < S5_ROWS:
        pr, pi = apow(float(S5_L * d))
        lp.append(jnp.stack([blk(pr), blk(pi)], axis=1))
        d *= 2
    lp_tab = jnp.stack(lp, axis=1)
    steps = (S5_L * (jnp.arange(S5_ROWS, dtype=F32) + 1.0))[:, None, None]
    wr, wi = apow(steps)
    w_tab = jnp.stack([blk(wr), blk(wi)], axis=1)
    return bb, cc, a_tab, lp_tab, w_tab


def _s5_core(u, bb, cc, a_tab, lp_tab, w_tab):
    S = u.shape[0]
    nt = S // S5_TT
    return pl.pallas_call(
        _s5_kernel,
        grid=(S5_NBLK, nt),
        in_specs=[
            pl.BlockSpec((S5_TT, LANES), lambda b, t: (t, 2 * b)),
            pl.BlockSpec((S5_TT, LANES), lambda b, t: (t, 2 * b + 1)),
            pl.BlockSpec((1, S5_BLK, 2 * S5_ST), lambda b, t: (b, 0, 0)),
            pl.BlockSpec((1, 2 * S5_ST, S5_BLK), lambda b, t: (b, 0, 0)),
            pl.BlockSpec((1, 2, S5_ST), lambda b, t: (b, 0, 0)),
            pl.BlockSpec((1,) + lp_tab.shape[1:], lambda b, t: (b, 0, 0, 0)),
            pl.BlockSpec((1, 2, S5_ROWS, S5_ST), lambda b, t: (b, 0, 0, 0)),
        ],
        out_specs=pl.BlockSpec((S5_TT, S5_BLK), lambda b, t: (t, b)),
        out_shape=jax.ShapeDtypeStruct((S, D_MODEL), F32),
        scratch_shapes=[pltpu.VMEM((S5_BLK // LANES, S5_TT, LANES), F32),
                        pltpu.VMEM((S5_TT, 2 * S5_ST), F32),
                        pltpu.VMEM((S5_TT, 2 * S5_ST), BF16),
                        pltpu.VMEM((S5_BLK // LANES, S5_TT, LANES), F32),
                        pltpu.VMEM((2, S5_ST), F32)],
        compiler_params=_params("parallel", "arbitrary"),
        name="s5_core",
    )(u, u, bb, cc, a_tab, lp_tab, w_tab)


def kernel(x, positions, mla_w_a, mla_g_q, mla_g_kv, mla_w_uq, mla_w_ukv, mla_w_o,
           ssm_w_in, ssm_lambda_re, ssm_lambda_im, ssm_log_dt, ssm_b_re, ssm_b_im,
           ssm_c_re, ssm_c_im, ssm_d, ssm_w_glu, ffn_w_up, ffn_conv_w, ffn_conv_b,
           ffn_w_down, g_mix, g_ffn, g_final):
    B, S, D = x.shape
    assert (B, S, D) == (1, SEQ, D_MODEL)
    h = x.reshape(S, D)
    pos = positions.reshape(S, 1)

    qt, k, vt = _mla_proj(h, pos, g_mix[0], mla_w_a[0], mla_g_q[0], mla_g_kv[0],
                         mla_w_uq[0], mla_w_ukv[0])
    o = _attention(qt, k, vt)
    w_up_all = ffn_w_up.astype(BF16)
    w_down_all = ffn_w_down.astype(BF16)
    h, u = _attn_out_ffn_s5_in(h, o, mla_w_o[0], g_ffn[0], w_up_all, ffn_conv_w[0],
                               ffn_conv_b[0], w_down_all, g_mix[1], ssm_w_in[0])

    tabs = _s5_tables(ssm_lambda_re[0], ssm_lambda_im[0], ssm_log_dt[0],
                      ssm_b_re[0], ssm_b_im[0], ssm_c_re[0], ssm_c_im[0])
    y = _s5_core(u, *tabs)
    h = _s5_glu_ffn_final(h, y, u, ssm_d[0], ssm_w_glu[0], g_ffn[1], w_up_all,
                          ffn_conv_w[1], ffn_conv_b[1], w_down_all, g_final)
    return h.reshape(B, S, D)
```

```python
import functools
import math

import jax
import jax.numpy as jnp
from jax import lax
from jax.experimental import pallas as pl
from jax.experimental.pallas import tpu as pltpu

F32 = jnp.float32
BF16 = jnp.bfloat16

D_MODEL = 1024
SEQ = 16384
EPS = 1e-6
CHUNK = 64
MLA_HEADS = 8
QK_NOPE = 128
QK_ROPE = 64
V_HEAD = 128
Q_LORA = 384
KV_LORA = 256
ROPE_THETA = 10000.0
SSM_GROUP = 16
SSM_GROUPS = 64
SSM_STATE = 64
D_FF = 2816
CONV_W = 3

LANES = 128
SUBLANES = 8
HEAD_PAD = 2 * LANES
BF16_ROWS = 16
VMEM_LIMIT = 56 * 1024 * 1024

PROJ_TS = 512
ATT_TQ = 1024
ATT_TK = 1024
FFN_TS = 512
FFN_SUB = 256
FFN_TAIL = SUBLANES
FFN_CF = 256
S5_TT = 1024
S5_ROWS = SUBLANES
S5_L = S5_TT // S5_ROWS
S5_CH = 16
S5_BLK = 2 * LANES
S5_NBLK = D_MODEL // S5_BLK
S5_ST = S5_BLK // SSM_GROUP * SSM_STATE
V_EXT = V_HEAD + BF16_ROWS
NEG_BIG = -1e30


def _rms(x, g):
    return x * lax.rsqrt(jnp.mean(x * x, axis=-1, keepdims=True) + EPS) * g


def _const_spec(shape):
    nd = len(shape)
    return pl.BlockSpec(shape, lambda *_: (0,) * nd, pipeline_mode=pl.Buffered(1))


def _params(*sem):
    return pltpu.CompilerParams(dimension_semantics=sem, vmem_limit_bytes=VMEM_LIMIT)


def _mla_proj_kernel(x_ref, pos_ref, gmix_ref, wa_ref, gq_ref, gkv_ref, wqt_ref,
                     wk_ref, wvt_ref, inv_ref, qt_ref, k_ref, vt_ref):
    hn = _rms(x_ref[...], gmix_ref[...])
    a = jnp.dot(hn.astype(BF16), wa_ref[...], preferred_element_type=F32)
    cq = _rms(a[:, :Q_LORA], gq_ref[...])
    ckv = _rms(a[:, Q_LORA:Q_LORA + KV_LORA], gkv_ref[...])

    half = LANES // 2
    nfreq = QK_ROPE // 2
    pack = LANES // nfreq
    blk = x_ref.shape[0] // pack
    lane = lax.broadcasted_iota(jnp.int32, (blk, LANES), 1)
    pos_p = pos_ref[(pack - 1) * blk:pack * blk, :]
    for b in range(pack - 2, -1, -1):
        pos_p = jnp.where(lane < (b + 1) * nfreq, pos_ref[b * blk:(b + 1) * blk, :], pos_p)
    ang = pos_p.astype(F32) * inv_ref[...]
    cos_p = jnp.cos(ang)
    sin_p = jnp.sin(ang)
    ca_blocks, sa_blocks = [], []
    for b in range(pack):
        shift = (LANES - b * nfreq) % LANES
        c0 = pltpu.roll(cos_p, shift, 1) if shift else cos_p
        s0 = pltpu.roll(sin_p, shift, 1) if shift else sin_p
        c1 = pltpu.roll(c0, nfreq, 1)
        s1 = pltpu.roll(s0, nfreq, 1)
        ca_blocks.append(jnp.where(lane < nfreq, c0, jnp.where(lane < 2 * nfreq, c1, 0.0)))
        sa_blocks.append(jnp.where(lane < nfreq, -s0, jnp.where(lane < 2 * nfreq, s1, 0.0)))
    ca = jnp.concatenate(ca_blocks, axis=0)
    sa = jnp.concatenate(sa_blocks, axis=0)

    kr_in = a[:, Q_LORA + KV_LORA:]
    kr = (kr_in * ca + pltpu.roll(kr_in, half, 1) * sa).astype(BF16)
    kn = jnp.dot(ckv.astype(BF16), wk_ref[...], preferred_element_type=F32)
    for h in range(MLA_HEADS):
        k_ref[h, :, 0:LANES] = kn[:, h * LANES:(h + 1) * LANES].astype(BF16)
        k_ref[h, :, LANES:HEAD_PAD] = kr

    ca_t = ca.T
    sa_t = sa.T
    scale = (QK_NOPE + QK_ROPE) ** -0.5 * math.log2(math.e)
    qt = jnp.dot(wqt_ref[...], cq.T.astype(BF16), preferred_element_type=F32)
    for h in range(MLA_HEADS):
        base = h * HEAD_PAD
        qt_ref[h, 0, 0:LANES, :] = (qt[base:base + LANES] * scale).astype(BF16)
        r_in = qt[base + LANES:base + HEAD_PAD]
        r_sw = jnp.concatenate([r_in[half:], r_in[:half]], axis=0)
        qt_ref[h, 0, LANES:HEAD_PAD, :] = ((r_in * ca_t + r_sw * sa_t) * scale).astype(BF16)
    vt = jnp.dot(wvt_ref[...], ckv.T.astype(BF16), preferred_element_type=F32)
    ones_grp = jnp.where(lax.broadcasted_iota(jnp.int32, (V_EXT - V_HEAD, vt.shape[1]), 0) == 0,
                         1.0, 0.0).astype(BF16)
    for h in range(MLA_HEADS):
        vt_ref[h, 0, 0:V_HEAD, :] = vt[h * V_HEAD:(h + 1) * V_HEAD].astype(BF16)
        vt_ref[h, 0, V_HEAD:V_EXT, :] = ones_grp


def _mla_proj(x, pos, g_mix, w_a, g_q, g_kv, w_uq, w_ukv):
    S = x.shape[0]
    H = MLA_HEADS
    ts = PROJ_TS
    kr = w_a[:, Q_LORA + KV_LORA:]
    kr_sw = jnp.concatenate([kr[:, QK_ROPE // 2:], kr[:, :QK_ROPE // 2]], axis=1)
    wa_ext = jnp.concatenate([w_a, kr_sw], axis=1).astype(BF16)
    wq = w_uq.reshape(Q_LORA, H, QK_NOPE + QK_ROPE)
    wq_r = wq[:, :, QK_NOPE:]
    wq_sw = jnp.concatenate([wq_r[:, :, QK_ROPE // 2:], wq_r[:, :, :QK_ROPE // 2]], axis=2)
    wq_t = jnp.concatenate([wq, wq_sw], axis=2).reshape(Q_LORA, H * HEAD_PAD).T.astype(BF16)
    wkv = w_ukv.reshape(KV_LORA, H, QK_NOPE + V_HEAD)
    wk = wkv[:, :, :QK_NOPE].reshape(KV_LORA, H * QK_NOPE).astype(BF16)
    wv_t = wkv[:, :, QK_NOPE:].reshape(KV_LORA, H * V_HEAD).T.astype(BF16)
    inv = 1.0 / (ROPE_THETA ** (jnp.arange(0, QK_ROPE, 2, dtype=F32) / QK_ROPE))
    inv_row = jnp.tile(inv, LANES // inv.shape[0])[None, :]

    n = S // ts
    per_q = ATT_TQ // ts
    per_kv = ATT_TK // ts
    row = lambda i: (i, 0)
    return pl.pallas_call(
        _mla_proj_kernel,
        grid=(n,),
        in_specs=[
            pl.BlockSpec((ts, D_MODEL), row),
            pl.BlockSpec((ts, 1), row),
            _const_spec((1, D_MODEL)),
            _const_spec(wa_ext.shape),
            _const_spec((1, Q_LORA)),
            _const_spec((1, KV_LORA)),
            _const_spec(wq_t.shape),
            _const_spec(wk.shape),
            _const_spec(wv_t.shape),
            _const_spec((1, LANES)),
        ],
        out_specs=[
            pl.BlockSpec((H, 1, HEAD_PAD, ts), lambda i: (0, i // per_q, 0, i % per_q)),
            pl.BlockSpec((H, ts, HEAD_PAD), lambda i: (0, i, 0)),
            pl.BlockSpec((H, 1, V_EXT, ts), lambda i: (0, i // per_kv, 0, i % per_kv)),
        ],
        out_shape=[
            jax.ShapeDtypeStruct((H, S // ATT_TQ, HEAD_PAD, ATT_TQ), BF16),
            jax.ShapeDtypeStruct((H, S, HEAD_PAD), BF16),
            jax.ShapeDtypeStruct((H, S // ATT_TK, V_EXT, ATT_TK), BF16),
        ],
        compiler_params=_params("parallel"),
        name="mla_proj",
    )(x, pos, g_mix[None, :], wa_ext, g_q[None, :], g_kv[None, :], wq_t, wk, wv_t, inv_row)


def _attn_kernel(qt_ref, k_ref, vt_ref, o_ref, sa_ref, sb_ref, m_ref, acc_ref):
    i = pl.program_id(1)
    m_ref[...] = jnp.full(m_ref.shape, NEG_BIG, F32)
    acc_ref[...] = jnp.zeros(acc_ref.shape, F32)
    groups = ATT_TK // SUBLANES

    def scores(j, dst):
        ks = pl.multiple_of(j * ATT_TK, ATT_TK)
        dst[...] = jnp.dot(k_ref[0, pl.ds(ks, ATT_TK), :], qt_ref[0, 0],
                           preferred_element_type=F32)

    def softmax_pv(src, j, masked):
        s = src[...]
        if masked:
            ck = lax.broadcasted_iota(jnp.int32, s.shape, 0) // CHUNK
            rq = (q_off + lax.broadcasted_iota(jnp.int32, s.shape, 1)) // CHUNK
            s = jnp.where(ck <= rq, s, NEG_BIG)
        m_prev = m_ref[...]
        m8 = jnp.max(s.reshape(groups, SUBLANES, ATT_TQ), axis=0)
        m_new = jnp.maximum(m_prev, jnp.max(m8, axis=0, keepdims=True))
        alpha = jnp.exp2(m_prev - m_new)
        p = jnp.exp2(s - m_new)
        pv = jnp.dot(vt_ref[0, j], p.astype(BF16), preferred_element_type=F32)
        acc_ref[...] = acc_ref[...] * alpha + pv
        m_ref[...] = m_new

    kv_per_q = ATT_TK // ATT_TQ
    n_full = i // kv_per_q
    q_off = (i % kv_per_q) * ATT_TQ

    scores(0, sa_ref)

    def pair(t, carry):
        j = 2 * t
        scores(j + 1, sb_ref)
        softmax_pv(sa_ref, j, False)
        scores(j + 2, sa_ref)
        softmax_pv(sb_ref, j + 1, False)
        return carry

    lax.fori_loop(0, n_full // 2, pair, 0)

    @pl.when(n_full % 2 == 0)
    def _():
        softmax_pv(sa_ref, n_full, True)

    @pl.when(n_full % 2 == 1)
    def _():
        scores(n_full, sb_ref)
        softmax_pv(sa_ref, n_full - 1, False)
        softmax_pv(sb_ref, n_full, True)

    acc = acc_ref[...]
    o_ref[...] = (acc[:V_HEAD] / acc[V_HEAD:V_HEAD + 1]).T.astype(o_ref.dtype)


def _attention(qt, k, vt):
    H, S, _ = k.shape
    nq = S // ATT_TQ
    nk = S // ATT_TK
    return pl.pallas_call(
        _attn_kernel,
        grid=(H, nq),
        in_specs=[
            pl.BlockSpec((1, 1, HEAD_PAD, ATT_TQ), lambda h, i: (h, i, 0, 0)),
            pl.BlockSpec((1, S, HEAD_PAD), lambda h, i: (h, 0, 0)),
            pl.BlockSpec((1, nk, V_EXT, ATT_TK), lambda h, i: (h, 0, 0, 0)),
        ],
        out_specs=pl.BlockSpec((ATT_TQ, V_HEAD), lambda h, i: (i, h)),
        out_shape=jax.ShapeDtypeStruct((S, H * V_HEAD), BF16),
        scratch_shapes=[
            pltpu.VMEM((ATT_TK, ATT_TQ), F32),
            pltpu.VMEM((ATT_TK, ATT_TQ), F32),
            pltpu.VMEM((1, ATT_TQ), F32),
            pltpu.VMEM((V_EXT, ATT_TQ), F32),
        ],
        compiler_params=_params("parallel", "arbitrary"),
        name="mla_attention",
    )(qt, k, vt)


def _ffn_act(h, rows, g_ref, wup_ref, cw_ref, cb_ref, act_ref, tail_ref):
    ts = h.shape[0]
    hn = _rms(h, g_ref[...]).astype(BF16)

    def conv_up(col):
        cols = slice(col, col + FFN_CF)
        up = jnp.dot(hn, wup_ref[:, cols], preferred_element_type=F32)
        ext = jnp.concatenate([tail_ref[:, cols], up], axis=0)
        tail_ref[:, cols] = up[ts - FFN_TAIL:, :]
        w = cw_ref[:, cols]
        return (ext[FFN_TAIL - 2:FFN_TAIL - 2 + ts] * w[0:1]
                + ext[FFN_TAIL - 1:FFN_TAIL - 1 + ts] * w[1:2]
                + up * w[2:3] + cb_ref[:, cols])

    for c in range(D_FF // FFN_CF):
        val = conv_up(c * FFN_CF)
        gate = conv_up(D_FF + c * FFN_CF)
        act_ref[rows, c * FFN_CF:(c + 1) * FFN_CF] = (
            gate * jax.nn.sigmoid(gate) * val).astype(BF16)


def _ffn_groups(i, tail_ref, pre, post, g_ref, wup_ref, cw_ref, cb_ref, wdn_ref, act_ref):
    @pl.when(i == 0)
    def _():
        tail_ref[...] = jnp.zeros(tail_ref.shape, F32)

    groups = [slice(r0, r0 + FFN_SUB) for r0 in range(0, FFN_TS, FFN_SUB)]
    hs = [pre(rows) for rows in groups]
    pending = None
    for rows, h in zip(groups, hs):
        _ffn_act(h, rows, g_ref, wup_ref, cw_ref, cb_ref, act_ref, tail_ref)
        if pending is not None:
            post(*pending)
        down = jnp.dot(act_ref[rows, :], wdn_ref[...], preferred_element_type=F32)
        pending = (rows, h + down)
    post(*pending)


def _ffn0_kernel(x_ref, o_ref, wo_ref, g_ref, wup_ref, cw_ref, cb_ref, wdn_ref, gmix_ref,
                 win_ref, h_out_ref, u_out_ref, act_ref, tail_ref):
    def pre(rows):
        return x_ref[rows, :] + jnp.dot(o_ref[rows, :], wo_ref[...], preferred_element_type=F32)

    def post(rows, h2):
        h_out_ref[rows, :] = h2
        u = jnp.dot(_rms(h2, gmix_ref[...]).astype(BF16), win_ref[...],
                    preferred_element_type=F32)
        for b in range(S5_NBLK):
            u_out_ref[b, rows, :] = u[:, b * S5_BLK:(b + 1) * S5_BLK]

    _ffn_groups(pl.program_id(0), tail_ref, pre, post, g_ref, wup_ref, cw_ref, cb_ref, wdn_ref,
                act_ref)


def _ffn1_kernel(h_ref, y_ref, u_ref, d_ref, wglu_ref, g_ref, wup_ref, cw_ref, cb_ref,
                 wdn_ref, gfin_ref, out_ref, act_ref, tail_ref):
    def pre(rows):
        y = jnp.concatenate([y_ref[b, rows, :] for b in range(S5_NBLK)], axis=1)
        u = jnp.concatenate([u_ref[b, rows, :] for b in range(S5_NBLK)], axis=1)
        yy = jax.nn.gelu(y + d_ref[...] * u)
        z = jnp.dot(yy.astype(BF16), wglu_ref[...], preferred_element_type=F32)
        return h_ref[rows, :] + z[:, :D_MODEL] * jax.nn.sigmoid(z[:, D_MODEL:])

    def post(rows, out):
        out_ref[rows, :] = _rms(out, gfin_ref[...])

    _ffn_groups(pl.program_id(0), tail_ref, pre, post, g_ref, wup_ref, cw_ref, cb_ref, wdn_ref,
                act_ref)


def _layer_spec(stacked, layer):
    return pl.BlockSpec((None,) + stacked.shape[1:], lambda *_: (layer, 0, 0),
                        pipeline_mode=pl.Buffered(1))


def _ffn_specs(w_up_all, conv_w, w_down_all, layer):
    return [_const_spec((1, D_MODEL)), _layer_spec(w_up_all, layer), _const_spec(conv_w.shape),
            _const_spec((1, 2 * D_FF)), _layer_spec(w_down_all, layer)]


def _blocked_tile():
    return pl.BlockSpec((S5_NBLK, FFN_TS, S5_BLK), lambda i: (0, i, 0))


def _ffn_scratch():
    return [pltpu.VMEM((FFN_TS, D_FF), BF16), pltpu.VMEM((FFN_TAIL, 2 * D_FF), F32)]


def _attn_out_ffn_s5_in(x, o, w_o, g_ffn, w_up_all, conv_w, conv_b, w_down_all, g_mix, w_in):
    S = x.shape[0]
    row = lambda i: (i, 0)
    tile = pl.BlockSpec((FFN_TS, D_MODEL), row)
    return pl.pallas_call(
        _ffn0_kernel,
        grid=(S // FFN_TS,),
        in_specs=[tile, tile, _const_spec(w_o.shape)] + _ffn_specs(w_up_all, conv_w, w_down_all, 0)
                 + [_const_spec((1, D_MODEL)), _const_spec(w_in.shape)],
        out_specs=[tile, _blocked_tile()],
        out_shape=[jax.ShapeDtypeStruct((S, D_MODEL), F32),
                   jax.ShapeDtypeStruct((S5_NBLK, S, S5_BLK), F32)],
        scratch_shapes=_ffn_scratch(),
        compiler_params=_params("arbitrary"),
        name="conv_ffn0",
    )(x, o, w_o.astype(BF16), g_ffn[None, :], w_up_all, conv_w, conv_b[None, :],
      w_down_all, g_mix[None, :], w_in.astype(BF16))


def _s5_glu_ffn_final(h, y, u, d, w_glu, g_ffn, w_up_all, conv_w, conv_b, w_down_all, g_final):
    S = h.shape[0]
    row = lambda i: (i, 0)
    tile = pl.BlockSpec((FFN_TS, D_MODEL), row)
    return pl.pallas_call(
        _ffn1_kernel,
        grid=(S // FFN_TS,),
        in_specs=[tile, _blocked_tile(), _blocked_tile(), _const_spec((1, D_MODEL)),
                  _const_spec(w_glu.shape)]
                 + _ffn_specs(w_up_all, conv_w, w_down_all, 1) + [_const_spec((1, D_MODEL))],
        out_specs=tile,
        out_shape=jax.ShapeDtypeStruct((S, D_MODEL), F32),
        scratch_shapes=_ffn_scratch(),
        compiler_params=_params("arbitrary"),
        name="conv_ffn1",
    )(h, y, u, d[None, :], w_glu.astype(BF16), g_ffn[None, :], w_up_all, conv_w,
      conv_b[None, :], w_down_all, g_final[None, :])


def _s5_kernel(u_ref, bb_ref, cc_ref, a_ref, lp_ref, w_ref, y_ref,
               up_ref, buf_ref, xb_ref, yp_ref, carry_ref):
    t = pl.program_id(1)
    nc = S5_ST // LANES
    pair = 2 * S5_ROWS

    @pl.when(t == 0)
    def _():
        carry_ref[...] = jnp.zeros(carry_ref.shape, F32)

    per_sub = S5_L // S5_ROWS
    ch_rows = S5_CH * S5_ROWS
    grp_per_ch = S5_CH // S5_ROWS
    halves = S5_BLK // LANES

    def chunk_groups(k):
        for m in range(k * grp_per_ch, (k + 1) * grp_per_ch):
            for r in range(S5_ROWS):
                yield r * per_sub + m, m * S5_ROWS * S5_ROWS + r

    def col(ref_row, c):
        return jnp.broadcast_to(ref_row[:, c * LANES:(c + 1) * LANES], (S5_ROWS, LANES))

    ar = [col(a_ref[0, 0:1, :], c) for c in range(nc)]
    ai = [col(a_ref[0, 1:2, :], c) for c in range(nc)]

    def rows(l):
        return slice(l * S5_ROWS, (l + 1) * S5_ROWS)

    def re_cols(c):
        return slice(c * LANES, (c + 1) * LANES)

    def im_cols(c):
        return slice(S5_ST + c * LANES, S5_ST + (c + 1) * LANES)

    x = [jnp.zeros((S5_ROWS, LANES), F32)] * (2 * nc)
    for k in range(S5_L // S5_CH):
        for q, start in chunk_groups(k):
            for h in range(halves):
                up_ref[h, pl.ds(start, S5_ROWS, stride=S5_ROWS), :] = (
                    u_ref[0, q * S5_ROWS:(q + 1) * S5_ROWS, h * LANES:(h + 1) * LANES])
        ck = slice(k * ch_rows, (k + 1) * ch_rows)
        up = jnp.concatenate([up_ref[h, ck, :] for h in range(halves)], axis=1).astype(BF16)
        buf_ref[ck, :] = jnp.dot(up, bb_ref[0], preferred_element_type=F32)
        for l in range(k * S5_CH, (k + 1) * S5_CH):
            for c in range(nc):
                xr, xi = x[2 * c], x[2 * c + 1]
                nr = ar[c] * xr - ai[c] * xi + buf_ref[rows(l), re_cols(c)]
                ni = ar[c] * xi + ai[c] * xr + buf_ref[rows(l), im_cols(c)]
                buf_ref[rows(l), re_cols(c)] = nr
                buf_ref[rows(l), im_cols(c)] = ni
                x[2 * c], x[2 * c + 1] = nr, ni
    ends = x

    sub = lax.broadcasted_iota(jnp.int32, (S5_ROWS, LANES), 0)
    prev = []
    for c in range(nc):
        sl = slice(c * LANES, (c + 1) * LANES)
        er, ei = ends[2 * c], ends[2 * c + 1]
        d = 1
        k = 0
        while d < S5_ROWS:
            pr = lp_ref[0, k, 0:1, sl]
            pi = lp_ref[0, k, 1:2, sl]
            sr = jnp.where(sub >= d, pltpu.roll(er, d, 0), 0.0)
            si = jnp.where(sub >= d, pltpu.roll(ei, d, 0), 0.0)
            er, ei = er + pr * sr - pi * si, ei + pr * si + pi * sr
            d *= 2
            k += 1
        cr = carry_ref[0:1, sl]
        ci = carry_ref[1:2, sl]
        wr = w_ref[0, 0, :, sl]
        wi = w_ref[0, 1, :, sl]
        er, ei = er + wr * cr - wi * ci, ei + wr * ci + wi * cr
        carry_ref[0:1, sl] = er[S5_ROWS - 1:S5_ROWS, :]
        carry_ref[1:2, sl] = ei[S5_ROWS - 1:S5_ROWS, :]
        prev += [jnp.where(sub >= 1, pltpu.roll(er, 1, 0), jnp.broadcast_to(cr, (S5_ROWS, LANES))),
                 jnp.where(sub >= 1, pltpu.roll(ei, 1, 0), jnp.broadcast_to(ci, (S5_ROWS, LANES)))]

    p = prev
    for k in range(S5_L // S5_CH):
        for j in range(k * S5_CH // 2, (k + 1) * S5_CH // 2):
            dst = slice(j * pair, (j + 1) * pair)
            for c in range(nc):
                p0r = ar[c] * p[2 * c] - ai[c] * p[2 * c + 1]
                p0i = ar[c] * p[2 * c + 1] + ai[c] * p[2 * c]
                p1r = ar[c] * p0r - ai[c] * p0i
                p1i = ar[c] * p0i + ai[c] * p0r
                xr = jnp.concatenate([buf_ref[rows(2 * j), re_cols(c)] + p0r,
                                      buf_ref[rows(2 * j + 1), re_cols(c)] + p1r], axis=0)
                xi = jnp.concatenate([buf_ref[rows(2 * j), im_cols(c)] + p0i,
                                      buf_ref[rows(2 * j + 1), im_cols(c)] + p1i], axis=0)
                xb_ref[dst, re_cols(c)] = xr.astype(BF16)
                xb_ref[dst, im_cols(c)] = xi.astype(BF16)
                p[2 * c], p[2 * c + 1] = p1r, p1i
        ck = slice(k * ch_rows, (k + 1) * ch_rows)
        yp = jnp.dot(xb_ref[ck, :], cc_ref[0], preferred_element_type=F32)
        for h in range(halves):
            yp_ref[h, ck, :] = yp[:, h * LANES:(h + 1) * LANES]
        for q, start in chunk_groups(k):
            for h in range(halves):
                y_ref[0, q * S5_ROWS:(q + 1) * S5_ROWS, h * LANES:(h + 1) * LANES] = (
                    yp_ref[h, pl.ds(start, S5_ROWS, stride=S5_ROWS), :])


def _s5_tables(lam_re, lam_im, log_dt, b_re, b_im, c_re, c_im):
    G, P, C = SSM_GROUPS, SSM_STATE, SSM_GROUP
    nb, gb = S5_NBLK, SSM_GROUPS // S5_NBLK
    dt = jnp.exp(log_dt.astype(F32))[:, None]
    lr = lam_re.astype(F32)
    li = lam_im.astype(F32)
    mag = jnp.exp(lr * dt)
    ar = mag * jnp.cos(li * dt)
    ai = mag * jnp.sin(li * dt)
    den = lr * lr + li * li
    nr = ar - 1.0
    coef_r = (nr * lr + ai * li) / den
    coef_i = (ai * lr - nr * li) / den
    br = b_re.astype(F32)
    bi = b_im.astype(F32)
    bbar_r = coef_r[..., None] * br - coef_i[..., None] * bi
    bbar_i = coef_r[..., None] * bi + coef_i[..., None] * br

    def blockdiag(m, rows_per_group, cols_per_group):
        m = m.reshape(nb, gb * rows_per_group, cols_per_group)
        tiled = jnp.tile(m, (1, 1, gb))
        row_g = lax.broadcasted_iota(jnp.int32, tiled.shape, 1) // rows_per_group
        col_g = lax.broadcasted_iota(jnp.int32, tiled.shape, 2) // cols_per_group
        return jnp.where(row_g == col_g, tiled, 0.0)

    def blockdiag_in(m):
        return blockdiag(jnp.swapaxes(m, 1, 2), C, P)

    def blockdiag_out(m):
        return blockdiag(jnp.swapaxes(m, 1, 2), P, C)

    bb = jnp.concatenate([blockdiag_in(bbar_r), blockdiag_in(bbar_i)], axis=2).astype(BF16)
    cc = jnp.concatenate([blockdiag_out(c_re.astype(F32)),
                          -blockdiag_out(c_im.astype(F32))], axis=1).astype(BF16)

    def apow(nsteps):
        m = jnp.exp(lr * dt * nsteps)
        return m * jnp.cos(li * dt * nsteps), m * jnp.sin(li * dt * nsteps)

    def blk(x):
        lead = x.shape[:-2]
        x = x.reshape(lead + (nb, gb * P))
        return jnp.moveaxis(x, -2, 0)

    a_tab = jnp.stack([blk(ar), blk(ai)], axis=1)
    lp = []
    d = 1
    while d < S5_ROWS:
        pr, pi = apow(float(S5_L * d))
        lp.append(jnp.stack([blk(pr), blk(pi)], axis=1))
        d *= 2
    lp_tab = jnp.stack(lp, axis=1)
    steps = (S5_L * (jnp.arange(S5_ROWS, dtype=F32) + 1.0))[:, None, None]
    wr, wi = apow(steps)
    w_tab = jnp.stack([blk(wr), blk(wi)], axis=1)
    return bb, cc, a_tab, lp_tab, w_tab


def _s5_core(u, bb, cc, a_tab, lp_tab, w_tab):
    S = u.shape[1]
    nt = S // S5_TT
    return pl.pallas_call(
        _s5_kernel,
        grid=(S5_NBLK, nt),
        in_specs=[
            pl.BlockSpec((1, S5_TT, S5_BLK), lambda b, t: (b, t, 0)),
            pl.BlockSpec((1, S5_BLK, 2 * S5_ST), lambda b, t: (b, 0, 0)),
            pl.BlockSpec((1, 2 * S5_ST, S5_BLK), lambda b, t: (b, 0, 0)),
            pl.BlockSpec((1, 2, S5_ST), lambda b, t: (b, 0, 0)),
            pl.BlockSpec((1,) + lp_tab.shape[1:], lambda b, t: (b, 0, 0, 0)),
            pl.BlockSpec((1, 2, S5_ROWS, S5_ST), lambda b, t: (b, 0, 0, 0)),
        ],
        out_specs=pl.BlockSpec((1, S5_TT, S5_BLK), lambda b, t: (b, t, 0)),
        out_shape=jax.ShapeDtypeStruct((S5_NBLK, S, S5_BLK), F32),
        scratch_shapes=[pltpu.VMEM((S5_BLK // LANES, S5_TT, LANES), F32),
                        pltpu.VMEM((S5_TT, 2 * S5_ST), F32),
                        pltpu.VMEM((S5_TT, 2 * S5_ST), BF16),
                        pltpu.VMEM((S5_BLK // LANES, S5_TT, LANES), F32),
                        pltpu.VMEM((2, S5_ST), F32)],
        compiler_params=_params("parallel", "arbitrary"),
        name="s5_core",
    )(u, bb, cc, a_tab, lp_tab, w_tab)


def kernel(x, positions, mla_w_a, mla_g_q, mla_g_kv, mla_w_uq, mla_w_ukv, mla_w_o,
           ssm_w_in, ssm_lambda_re, ssm_lambda_im, ssm_log_dt, ssm_b_re, ssm_b_im,
           ssm_c_re, ssm_c_im, ssm_d, ssm_w_glu, ffn_w_up, ffn_conv_w, ffn_conv_b,
           ffn_w_down, g_mix, g_ffn, g_final):
    B, S, D = x.shape
    assert (B, S, D) == (1, SEQ, D_MODEL)
    h = x.reshape(S, D)
    pos = positions.reshape(S, 1)

    qt, k, vt = _mla_proj(h, pos, g_mix[0], mla_w_a[0], mla_g_q[0], mla_g_kv[0],
                         mla_w_uq[0], mla_w_ukv[0])
    o = _attention(qt, k, vt)
    w_up_all = ffn_w_up.astype(BF16)
    w_down_all = ffn_w_down.astype(BF16)
    h, u = _attn_out_ffn_s5_in(h, o, mla_w_o[0], g_ffn[0], w_up_all, ffn_conv_w[0],
                               ffn_conv_b[0], w_down_all, g_mix[1], ssm_w_in[0])

    tabs = _s5_tables(ssm_lambda_re[0], ssm_lambda_im[0], ssm_log_dt[0],
                      ssm_b_re[0], ssm_b_im[0], ssm_c_re[0], ssm_c_im[0])
    y = _s5_core(u, *tabs)
    h = _s5_glu_ffn_final(h, y, u, ssm_d[0], ssm_w_glu[0], g_ffn[1], w_up_all,
                          ffn_conv_w[1], ffn_conv_b[1], w_down_all, g_final)
    return h.reshape(B, S, D)
```

```python
import functools
import math

import jax
import jax.numpy as jnp
from jax import lax
from jax.experimental import pallas as pl
from jax.experimental.pallas import tpu as pltpu

F32 = jnp.float32
BF16 = jnp.bfloat16

D_MODEL = 1024
SEQ = 16384
EPS = 1e-6
CHUNK = 64
MLA_HEADS = 8
QK_NOPE = 128
QK_ROPE = 64
V_HEAD = 128
Q_LORA = 384
KV_LORA = 256
ROPE_THETA = 10000.0
SSM_GROUP = 16
SSM_GROUPS = 64
SSM_STATE = 64
D_FF = 2816
CONV_W = 3

LANES = 128
SUBLANES = 8
HEAD_PAD = 2 * LANES
BF16_ROWS = 16
VMEM_LIMIT = 56 * 1024 * 1024

PROJ_TS = 512
ATT_TQ = 1024
ATT_TK = 1024
FFN_TS = 512
FFN_SUB = 256
FFN_TAIL = SUBLANES
FFN_CF = 256
S5_TT = 512
S5_ROWS = SUBLANES
S5_L = S5_TT // S5_ROWS
S5_CH = 16
S5_BLK = 2 * LANES
S5_NBLK = D_MODEL // S5_BLK
S5_ST = S5_BLK // SSM_GROUP * SSM_STATE
V_EXT = V_HEAD + BF16_ROWS
NEG_BIG = -1e30


def _rms(x, g):
    return x * lax.rsqrt(jnp.mean(x * x, axis=-1, keepdims=True) + EPS) * g


def _const_spec(shape):
    nd = len(shape)
    return pl.BlockSpec(shape, lambda *_: (0,) * nd, pipeline_mode=pl.Buffered(1))


def _params(*sem):
    return pltpu.CompilerParams(dimension_semantics=sem, vmem_limit_bytes=VMEM_LIMIT)


def _mla_proj_kernel(x_ref, pos_ref, gmix_ref, wa_ref, gq_ref, gkv_ref, wqt_ref,
                     wk_ref, wvt_ref, inv_ref, qt_ref, k_ref, vt_ref):
    hn = _rms(x_ref[...], gmix_ref[...])
    a = jnp.dot(hn.astype(BF16), wa_ref[...], preferred_element_type=F32)
    cq = _rms(a[:, :Q_LORA], gq_ref[...])
    ckv = _rms(a[:, Q_LORA:Q_LORA + KV_LORA], gkv_ref[...])

    half = LANES // 2
    nfreq = QK_ROPE // 2
    pack = LANES // nfreq
    blk = x_ref.shape[0] // pack
    lane = lax.broadcasted_iota(jnp.int32, (blk, LANES), 1)
    pos_p = pos_ref[(pack - 1) * blk:pack * blk, :]
    for b in range(pack - 2, -1, -1):
        pos_p = jnp.where(lane < (b + 1) * nfreq, pos_ref[b * blk:(b + 1) * blk, :], pos_p)
    ang = pos_p.astype(F32) * inv_ref[...]
    cos_p = jnp.cos(ang)
    sin_p = jnp.sin(ang)
    ca_blocks, sa_blocks = [], []
    for b in range(pack):
        shift = (LANES - b * nfreq) % LANES
        c0 = pltpu.roll(cos_p, shift, 1) if shift else cos_p
        s0 = pltpu.roll(sin_p, shift, 1) if shift else sin_p
        c1 = pltpu.roll(c0, nfreq, 1)
        s1 = pltpu.roll(s0, nfreq, 1)
        ca_blocks.append(jnp.where(lane < nfreq, c0, jnp.where(lane < 2 * nfreq, c1, 0.0)))
        sa_blocks.append(jnp.where(lane < nfreq, -s0, jnp.where(lane < 2 * nfreq, s1, 0.0)))
    ca = jnp.concatenate(ca_blocks, axis=0)
    sa = jnp.concatenate(sa_blocks, axis=0)

    kr_in = a[:, Q_LORA + KV_LORA:]
    kr = (kr_in * ca + pltpu.roll(kr_in, half, 1) * sa).astype(BF16)
    kn = jnp.dot(ckv.astype(BF16), wk_ref[...], preferred_element_type=F32)
    for h in range(MLA_HEADS):
        k_ref[h, :, 0:LANES] = kn[:, h * LANES:(h + 1) * LANES].astype(BF16)
        k_ref[h, :, LANES:HEAD_PAD] = kr

    ca_t = ca.T
    sa_t = sa.T
    scale = (QK_NOPE + QK_ROPE) ** -0.5 * math.log2(math.e)
    qt = jnp.dot(wqt_ref[...], cq.T.astype(BF16), preferred_element_type=F32)
    for h in range(MLA_HEADS):
        base = h * HEAD_PAD
        qt_ref[h, 0, 0:LANES, :] = (qt[base:base + LANES] * scale).astype(BF16)
        r_in = qt[base + LANES:base + HEAD_PAD]
        r_sw = jnp.concatenate([r_in[half:], r_in[:half]], axis=0)
        qt_ref[h, 0, LANES:HEAD_PAD, :] = ((r_in * ca_t + r_sw * sa_t) * scale).astype(BF16)
    vt = jnp.dot(wvt_ref[...], ckv.T.astype(BF16), preferred_element_type=F32)
    ones_grp = jnp.where(lax.broadcasted_iota(jnp.int32, (V_EXT - V_HEAD, vt.shape[1]), 0) == 0,
                         1.0, 0.0).astype(BF16)
    for h in range(MLA_HEADS):
        vt_ref[h, 0, 0:V_HEAD, :] = vt[h * V_HEAD:(h + 1) * V_HEAD].astype(BF16)
        vt_ref[h, 0, V_HEAD:V_EXT, :] = ones_grp


def _mla_proj(x, pos, g_mix, w_a, g_q, g_kv, w_uq, w_ukv):
    S = x.shape[0]
    H = MLA_HEADS
    ts = PROJ_TS
    kr = w_a[:, Q_LORA + KV_LORA:]
    kr_sw = jnp.concatenate([kr[:, QK_ROPE // 2:], kr[:, :QK_ROPE // 2]], axis=1)
    wa_ext = jnp.concatenate([w_a, kr_sw], axis=1).astype(BF16)
    wq = w_uq.reshape(Q_LORA, H, QK_NOPE + QK_ROPE)
    wq_r = wq[:, :, QK_NOPE:]
    wq_sw = jnp.concatenate([wq_r[:, :, QK_ROPE // 2:], wq_r[:, :, :QK_ROPE // 2]], axis=2)
    wq_t = jnp.concatenate([wq, wq_sw], axis=2).reshape(Q_LORA, H * HEAD_PAD).T.astype(BF16)
    wkv = w_ukv.reshape(KV_LORA, H, QK_NOPE + V_HEAD)
    wk = wkv[:, :, :QK_NOPE].reshape(KV_LORA, H * QK_NOPE).astype(BF16)
    wv_t = wkv[:, :, QK_NOPE:].reshape(KV_LORA, H * V_HEAD).T.astype(BF16)
    inv = 1.0 / (ROPE_THETA ** (jnp.arange(0, QK_ROPE, 2, dtype=F32) / QK_ROPE))
    inv_row = jnp.tile(inv, LANES // inv.shape[0])[None, :]

    n = S // ts
    per_q = ATT_TQ // ts
    per_kv = ATT_TK // ts
    row = lambda i: (i, 0)
    return pl.pallas_call(
        _mla_proj_kernel,
        grid=(n,),
        in_specs=[
            pl.BlockSpec((ts, D_MODEL), row),
            pl.BlockSpec((ts, 1), row),
            _const_spec((1, D_MODEL)),
            _const_spec(wa_ext.shape),
            _const_spec((1, Q_LORA)),
            _const_spec((1, KV_LORA)),
            _const_spec(wq_t.shape),
            _const_spec(wk.shape),
            _const_spec(wv_t.shape),
            _const_spec((1, LANES)),
        ],
        out_specs=[
            pl.BlockSpec((H, 1, HEAD_PAD, ts), lambda i: (0, i // per_q, 0, i % per_q)),
            pl.BlockSpec((H, ts, HEAD_PAD), lambda i: (0, i, 0)),
            pl.BlockSpec((H, 1, V_EXT, ts), lambda i: (0, i // per_kv, 0, i % per_kv)),
        ],
        out_shape=[
            jax.ShapeDtypeStruct((H, S // ATT_TQ, HEAD_PAD, ATT_TQ), BF16),
            jax.ShapeDtypeStruct((H, S, HEAD_PAD), BF16),
            jax.ShapeDtypeStruct((H, S // ATT_TK, V_EXT, ATT_TK), BF16),
        ],
        compiler_params=_params("parallel"),
        name="mla_proj",
    )(x, pos, g_mix[None, :], wa_ext, g_q[None, :], g_kv[None, :], wq_t, wk, wv_t, inv_row)


def _attn_kernel(qt_ref, k_ref, vt_ref, o_ref, sa_ref, sb_ref, m_ref, acc_ref):
    i = pl.program_id(1)
    m_ref[...] = jnp.full(m_ref.shape, NEG_BIG, F32)
    acc_ref[...] = jnp.zeros(acc_ref.shape, F32)
    groups = ATT_TK // SUBLANES

    def scores(j, dst):
        ks = pl.multiple_of(j * ATT_TK, ATT_TK)
        dst[...] = jnp.dot(k_ref[0, pl.ds(ks, ATT_TK), :], qt_ref[0, 0],
                           preferred_element_type=F32)

    def softmax_pv(src, j, masked):
        s = src[...]
        if masked:
            ck = lax.broadcasted_iota(jnp.int32, s.shape, 0) // CHUNK
            rq = (q_off + lax.broadcasted_iota(jnp.int32, s.shape, 1)) // CHUNK
            s = jnp.where(ck <= rq, s, NEG_BIG)
        m_prev = m_ref[...]
        m8 = jnp.max(s.reshape(groups, SUBLANES, ATT_TQ), axis=0)
        m_new = jnp.maximum(m_prev, jnp.max(m8, axis=0, keepdims=True))
        alpha = jnp.exp2(m_prev - m_new)
        p = jnp.exp2(s - m_new)
        pv = jnp.dot(vt_ref[0, j], p.astype(BF16), preferred_element_type=F32)
        acc_ref[...] = acc_ref[...] * alpha + pv
        m_ref[...] = m_new

    kv_per_q = ATT_TK // ATT_TQ
    n_full = i // kv_per_q
    q_off = (i % kv_per_q) * ATT_TQ

    scores(0, sa_ref)

    def pair(t, carry):
        j = 2 * t
        scores(j + 1, sb_ref)
        softmax_pv(sa_ref, j, False)
        scores(j + 2, sa_ref)
        softmax_pv(sb_ref, j + 1, False)
        return carry

    lax.fori_loop(0, n_full // 2, pair, 0)

    @pl.when(n_full % 2 == 0)
    def _():
        softmax_pv(sa_ref, n_full, True)

    @pl.when(n_full % 2 == 1)
    def _():
        scores(n_full, sb_ref)
        softmax_pv(sa_ref, n_full - 1, False)
        softmax_pv(sb_ref, n_full, True)

    acc = acc_ref[...]
    o_ref[...] = (acc[:V_HEAD] / acc[V_HEAD:V_HEAD + 1]).T.astype(o_ref.dtype)


def _attention(qt, k, vt):
    H, S, _ = k.shape
    nq = S // ATT_TQ
    nk = S // ATT_TK
    return pl.pallas_call(
        _attn_kernel,
        grid=(H, nq),
        in_specs=[
            pl.BlockSpec((1, 1, HEAD_PAD, ATT_TQ), lambda h, i: (h, i, 0, 0)),
            pl.BlockSpec((1, S, HEAD_PAD), lambda h, i: (h, 0, 0)),
            pl.BlockSpec((1, nk, V_EXT, ATT_TK), lambda h, i: (h, 0, 0, 0)),
        ],
        out_specs=pl.BlockSpec((ATT_TQ, V_HEAD), lambda h, i: (i, h)),
        out_shape=jax.ShapeDtypeStruct((S, H * V_HEAD), BF16),
        scratch_shapes=[
            pltpu.VMEM((ATT_TK, ATT_TQ), F32),
            pltpu.VMEM((ATT_TK, ATT_TQ), F32),
            pltpu.VMEM((1, ATT_TQ), F32),
            pltpu.VMEM((V_EXT, ATT_TQ), F32),
        ],
        compiler_params=_params("parallel", "arbitrary"),
        name="mla_attention",
    )(qt, k, vt)


def _ffn_act(h, rows, g_ref, wup_ref, cw_ref, cb_ref, act_ref, tail_ref):
    ts = h.shape[0]
    hn = _rms(h, g_ref[...]).astype(BF16)

    def conv_up(col):
        cols = slice(col, col + FFN_CF)
        up = jnp.dot(hn, wup_ref[:, cols], preferred_element_type=F32)
        ext = jnp.concatenate([tail_ref[:, cols], up], axis=0)
        tail_ref[:, cols] = up[ts - FFN_TAIL:, :]
        w = cw_ref[:, cols]
        return (ext[FFN_TAIL - 2:FFN_TAIL - 2 + ts] * w[0:1]
                + ext[FFN_TAIL - 1:FFN_TAIL - 1 + ts] * w[1:2]
                + up * w[2:3] + cb_ref[:, cols])

    for c in range(D_FF // FFN_CF):
        val = conv_up(c * FFN_CF)
        gate = conv_up(D_FF + c * FFN_CF)
        act_ref[rows, c * FFN_CF:(c + 1) * FFN_CF] = (
            gate * jax.nn.sigmoid(gate) * val).astype(BF16)


def _ffn_groups(i, tail_ref, pre, post, g_ref, wup_ref, cw_ref, cb_ref, wdn_ref, act_ref):
    @pl.when(i == 0)
    def _():
        tail_ref[...] = jnp.zeros(tail_ref.shape, F32)

    groups = [slice(r0, r0 + FFN_SUB) for r0 in range(0, FFN_TS, FFN_SUB)]
    hs = [pre(rows) for rows in groups]
    pending = None
    for rows, h in zip(groups, hs):
        _ffn_act(h, rows, g_ref, wup_ref, cw_ref, cb_ref, act_ref, tail_ref)
        if pending is not None:
            post(*pending)
        down = jnp.dot(act_ref[rows, :], wdn_ref[...], preferred_element_type=F32)
        pending = (rows, h + down)
    post(*pending)


def _ffn0_kernel(x_ref, o_ref, wo_ref, g_ref, wup_ref, cw_ref, cb_ref, wdn_ref, gmix_ref,
                 win_ref, h_out_ref, u_out_ref, act_ref, tail_ref):
    def pre(rows):
        return x_ref[rows, :] + jnp.dot(o_ref[rows, :], wo_ref[...], preferred_element_type=F32)

    def post(rows, h2):
        h_out_ref[rows, :] = h2
        u_out_ref[rows, :] = jnp.dot(_rms(h2, gmix_ref[...]).astype(BF16), win_ref[...],
                                     preferred_element_type=F32)

    _ffn_groups(pl.program_id(0), tail_ref, pre, post, g_ref, wup_ref, cw_ref, cb_ref, wdn_ref,
                act_ref)


def _ffn1_kernel(h_ref, y_ref, u_ref, d_ref, wglu_ref, g_ref, wup_ref, cw_ref, cb_ref,
                 wdn_ref, gfin_ref, out_ref, act_ref, tail_ref):
    def pre(rows):
        yy = jax.nn.gelu(y_ref[rows, :] + d_ref[...] * u_ref[rows, :])
        z = jnp.dot(yy.astype(BF16), wglu_ref[...], preferred_element_type=F32)
        return h_ref[rows, :] + z[:, :D_MODEL] * jax.nn.sigmoid(z[:, D_MODEL:])

    def post(rows, out):
        out_ref[rows, :] = _rms(out, gfin_ref[...])

    _ffn_groups(pl.program_id(0), tail_ref, pre, post, g_ref, wup_ref, cw_ref, cb_ref, wdn_ref,
                act_ref)


def _layer_spec(stacked, layer):
    return pl.BlockSpec((None,) + stacked.shape[1:], lambda *_: (layer, 0, 0),
                        pipeline_mode=pl.Buffered(1))


def _ffn_specs(w_up_all, conv_w, w_down_all, layer):
    return [_const_spec((1, D_MODEL)), _layer_spec(w_up_all, layer), _const_spec(conv_w.shape),
            _const_spec((1, 2 * D_FF)), _layer_spec(w_down_all, layer)]


def _ffn_scratch():
    return [pltpu.VMEM((FFN_TS, D_FF), BF16), pltpu.VMEM((FFN_TAIL, 2 * D_FF), F32)]


def _attn_out_ffn_s5_in(x, o, w_o, g_ffn, w_up_all, conv_w, conv_b, w_down_all, g_mix, w_in):
    S = x.shape[0]
    row = lambda i: (i, 0)
    tile = pl.BlockSpec((FFN_TS, D_MODEL), row)
    return pl.pallas_call(
        _ffn0_kernel,
        grid=(S // FFN_TS,),
        in_specs=[tile, tile, _const_spec(w_o.shape)] + _ffn_specs(w_up_all, conv_w, w_down_all, 0)
                 + [_const_spec((1, D_MODEL)), _const_spec(w_in.shape)],
        out_specs=[tile, tile],
        out_shape=[jax.ShapeDtypeStruct((S, D_MODEL), F32)] * 2,
        scratch_shapes=_ffn_scratch(),
        compiler_params=_params("arbitrary"),
        name="conv_ffn0",
    )(x, o, w_o.astype(BF16), g_ffn[None, :], w_up_all, conv_w, conv_b[None, :],
      w_down_all, g_mix[None, :], w_in.astype(BF16))


def _s5_glu_ffn_final(h, y, u, d, w_glu, g_ffn, w_up_all, conv_w, conv_b, w_down_all, g_final):
    S = h.shape[0]
    row = lambda i: (i, 0)
    tile = pl.BlockSpec((FFN_TS, D_MODEL), row)
    return pl.pallas_call(
        _ffn1_kernel,
        grid=(S // FFN_TS,),
        in_specs=[tile, tile, tile, _const_spec((1, D_MODEL)), _const_spec(w_glu.shape)]
                 + _ffn_specs(w_up_all, conv_w, w_down_all, 1) + [_const_spec((1, D_MODEL))],
        out_specs=tile,
        out_shape=jax.ShapeDtypeStruct((S, D_MODEL), F32),
        scratch_shapes=_ffn_scratch(),
        compiler_params=_params("arbitrary"),
        name="conv_ffn1",
    )(h, y, u, d[None, :], w_glu.astype(BF16), g_ffn[None, :], w_up_all, conv_w,
      conv_b[None, :], w_down_all, g_final[None, :])


def _s5_kernel(ulo_ref, uhi_ref, bb_ref, cc_ref, a_ref, lp_ref, w_ref, y_ref,
               up_ref, buf_ref, xb_ref, yp_ref, carry_ref):
    t = pl.program_id(1)
    nc = S5_ST // LANES
    pair = 2 * S5_ROWS

    @pl.when(t == 0)
    def _():
        carry_ref[...] = jnp.zeros(carry_ref.shape, F32)

    per_sub = S5_L // S5_ROWS
    ch_rows = S5_CH * S5_ROWS
    grp_per_ch = S5_CH // S5_ROWS
    halves = S5_BLK // LANES

    def chunk_groups(k):
        for m in range(k * grp_per_ch, (k + 1) * grp_per_ch):
            for r in range(S5_ROWS):
                yield r * per_sub + m, m * S5_ROWS * S5_ROWS + r

    def col(ref_row, c):
        return jnp.broadcast_to(ref_row[:, c * LANES:(c + 1) * LANES], (S5_ROWS, LANES))

    ar = [col(a_ref[0, 0:1, :], c) for c in range(nc)]
    ai = [col(a_ref[0, 1:2, :], c) for c in range(nc)]

    def rows(l):
        return slice(l * S5_ROWS, (l + 1) * S5_ROWS)

    def re_cols(c):
        return slice(c * LANES, (c + 1) * LANES)

    def im_cols(c):
        return slice(S5_ST + c * LANES, S5_ST + (c + 1) * LANES)

    x = [jnp.zeros((S5_ROWS, LANES), F32)] * (2 * nc)
    for k in range(S5_L // S5_CH):
        for q, start in chunk_groups(k):
            for h, src in enumerate((ulo_ref, uhi_ref)):
                up_ref[h, pl.ds(start, S5_ROWS, stride=S5_ROWS), :] = (
                    src[q * S5_ROWS:(q + 1) * S5_ROWS, :])
        ck = slice(k * ch_rows, (k + 1) * ch_rows)
        up = jnp.concatenate([up_ref[h, ck, :] for h in range(halves)], axis=1).astype(BF16)
        buf_ref[ck, :] = jnp.dot(up, bb_ref[0], preferred_element_type=F32)
        for l in range(k * S5_CH, (k + 1) * S5_CH):
            for c in range(nc):
                xr, xi = x[2 * c], x[2 * c + 1]
                nr = ar[c] * xr - ai[c] * xi + buf_ref[rows(l), re_cols(c)]
                ni = ar[c] * xi + ai[c] * xr + buf_ref[rows(l), im_cols(c)]
                buf_ref[rows(l), re_cols(c)] = nr
                buf_ref[rows(l), im_cols(c)] = ni
                x[2 * c], x[2 * c + 1] = nr, ni
    ends = x

    sub = lax.broadcasted_iota(jnp.int32, (S5_ROWS, LANES), 0)
    prev = []
    for c in range(nc):
        sl = slice(c * LANES, (c + 1) * LANES)
        er, ei = ends[2 * c], ends[2 * c + 1]
        d = 1
        k = 0
        while d < S5_ROWS:
            pr = lp_ref[0, k, 0:1, sl]
            pi = lp_ref[0, k, 1:2, sl]
            sr = jnp.where(sub >= d, pltpu.roll(er, d, 0), 0.0)
            si = jnp.where(sub >= d, pltpu.roll(ei, d, 0), 0.0)
            er, ei = er + pr * sr - pi * si, ei + pr * si + pi * sr
            d *= 2
            k += 1
        cr = carry_ref[0:1, sl]
        ci = carry_ref[1:2, sl]
        wr = w_ref[0, 0, :, sl]
        wi = w_ref[0, 1, :, sl]
        er, ei = er + wr * cr - wi * ci, ei + wr * ci + wi * cr
        carry_ref[0:1, sl] = er[S5_ROWS - 1:S5_ROWS, :]
        carry_ref[1:2, sl] = ei[S5_ROWS - 1:S5_ROWS, :]
        prev += [jnp.where(sub >= 1, pltpu.roll(er, 1, 0), jnp.broadcast_to(cr, (S5_ROWS, LANES))),
                 jnp.where(sub >= 1, pltpu.roll(ei, 1, 0), jnp.broadcast_to(ci, (S5_ROWS, LANES)))]

    p = prev
    for k in range(S5_L // S5_CH):
        for j in range(k * S5_CH // 2, (k + 1) * S5_CH // 2):
            dst = slice(j * pair, (j + 1) * pair)
            for c in range(nc):
                p0r = ar[c] * p[2 * c] - ai[c] * p[2 * c + 1]
                p0i = ar[c] * p[2 * c + 1] + ai[c] * p[2 * c]
                p1r = ar[c] * p0r - ai[c] * p0i
                p1i = ar[c] * p0i + ai[c] * p0r
                xr = jnp.concatenate([buf_ref[rows(2 * j), re_cols(c)] + p0r,
                                      buf_ref[rows(2 * j + 1), re_cols(c)] + p1r], axis=0)
                xi = jnp.concatenate([buf_ref[rows(2 * j), im_cols(c)] + p0i,
                                      buf_ref[rows(2 * j + 1), im_cols(c)] + p1i], axis=0)
                xb_ref[dst, re_cols(c)] = xr.astype(BF16)
                xb_ref[dst, im_cols(c)] = xi.astype(BF16)
                p[2 * c], p[2 * c + 1] = p1r, p1i
        ck = slice(k * ch_rows, (k + 1) * ch_rows)
        yp = jnp.dot(xb_ref[ck, :], cc_ref[0], preferred_element_type=F32)
        for h in range(halves):
            yp_ref[h, ck, :] = yp[:, h * LANES:(h + 1) * LANES]
        for q, start in chunk_groups(k):
            for h in range(halves):
                y_ref[q * S5_ROWS:(q + 1) * S5_ROWS, h * LANES:(h + 1) * LANES] = (
                    yp_ref[h, pl.ds(start, S5_ROWS, stride=S5_ROWS), :])


def _s5_tables(lam_re, lam_im, log_dt, b_re, b_im, c_re, c_im):
    G, P, C = SSM_GROUPS, SSM_STATE, SSM_GROUP
    nb, gb = S5_NBLK, SSM_GROUPS // S5_NBLK
    dt = jnp.exp(log_dt.astype(F32))[:, None]
    lr = lam_re.astype(F32)
    li = lam_im.astype(F32)
    mag = jnp.exp(lr * dt)
    ar = mag * jnp.cos(li * dt)
    ai = mag * jnp.sin(li * dt)
    den = lr * lr + li * li
    nr = ar - 1.0
    coef_r = (nr * lr + ai * li) / den
    coef_i = (ai * lr - nr * li) / den
    br = b_re.astype(F32)
    bi = b_im.astype(F32)
    bbar_r = coef_r[..., None] * br - coef_i[..., None] * bi
    bbar_i = coef_r[..., None] * bi + coef_i[..., None] * br

    def blockdiag(m, rows_per_group, cols_per_group):
        m = m.reshape(nb, gb * rows_per_group, cols_per_group)
        tiled = jnp.tile(m, (1, 1, gb))
        row_g = lax.broadcasted_iota(jnp.int32, tiled.shape, 1) // rows_per_group
        col_g = lax.broadcasted_iota(jnp.int32, tiled.shape, 2) // cols_per_group
        return jnp.where(row_g == col_g, tiled, 0.0)

    def blockdiag_in(m):
        return blockdiag(jnp.swapaxes(m, 1, 2), C, P)

    def blockdiag_out(m):
        return blockdiag(jnp.swapaxes(m, 1, 2), P, C)

    bb = jnp.concatenate([blockdiag_in(bbar_r), blockdiag_in(bbar_i)], axis=2).astype(BF16)
    cc = jnp.concatenate([blockdiag_out(c_re.astype(F32)),
                          -blockdiag_out(c_im.astype(F32))], axis=1).astype(BF16)

    def apow(nsteps):
        m = jnp.exp(lr * dt * nsteps)
        return m * jnp.cos(li * dt * nsteps), m * jnp.sin(li * dt * nsteps)

    def blk(x):
        lead = x.shape[:-2]
        x = x.reshape(lead + (nb, gb * P))
        return jnp.moveaxis(x, -2, 0)

    a_tab = jnp.stack([blk(ar), blk(ai)], axis=1)
    lp = []
    d = 1
    while d < S5_ROWS:
        pr, pi = apow(float(S5_L * d))
        lp.append(jnp.stack([blk(pr), blk(pi)], axis=1))
        d *= 2
    lp_tab = jnp.stack(lp, axis=1)
    steps = (S5_L * (jnp.arange(S5_ROWS, dtype=F32) + 1.0))[:, None, None]
    wr, wi = apow(steps)
    w_tab = jnp.stack([blk(wr), blk(wi)], axis=1)
    return bb, cc, a_tab, lp_tab, w_tab


def _s5_core(u, bb, cc, a_tab, lp_tab, w_tab):
    S = u.shape[0]
    nt = S // S5_TT
    return pl.pallas_call(
        _s5_kernel,
        grid=(S5_NBLK, nt),
        in_specs=[
            pl.BlockSpec((S5_TT, LANES), lambda b, t: (t, 2 * b)),
            pl.BlockSpec((S5_TT, LANES), lambda b, t: (t, 2 * b + 1)),
            pl.BlockSpec((1, S5_BLK, 2 * S5_ST), lambda b, t: (b, 0, 0)),
            pl.BlockSpec((1, 2 * S5_ST, S5_BLK), lambda b, t: (b, 0, 0)),
            pl.BlockSpec((1, 2, S5_ST), lambda b, t: (b, 0, 0)),
            pl.BlockSpec((1,) + lp_tab.shape[1:], lambda b, t: (b, 0, 0, 0)),
            pl.BlockSpec((1, 2, S5_ROWS, S5_ST), lambda b, t: (b, 0, 0, 0)),
        ],
        out_specs=pl.BlockSpec((S5_TT, S5_BLK), lambda b, t: (t, b)),
        out_shape=jax.ShapeDtypeStruct((S, D_MODEL), F32),
        scratch_shapes=[pltpu.VMEM((S5_BLK // LANES, S5_TT, LANES), F32),
                        pltpu.VMEM((S5_TT, 2 * S5_ST), F32),
                        pltpu.VMEM((S5_TT, 2 * S5_ST), BF16),
                        pltpu.VMEM((S5_BLK // LANES, S5_TT, LANES), F32),
                        pltpu.VMEM((2, S5_ST), F32)],
        compiler_params=_params("parallel", "arbitrary"),
        name="s5_core",
    )(u, u, bb, cc, a_tab, lp_tab, w_tab)


def kernel(x, positions, mla_w_a, mla_g_q, mla_g_kv, mla_w_uq, mla_w_ukv, mla_w_o,
           ssm_w_in, ssm_lambda_re, ssm_lambda_im, ssm_log_dt, ssm_b_re, ssm_b_im,
           ssm_c_re, ssm_c_im, ssm_d, ssm_w_glu, ffn_w_up, ffn_conv_w, ffn_conv_b,
           ffn_w_down, g_mix, g_ffn, g_final):
    B, S, D = x.shape
    assert (B, S, D) == (1, SEQ, D_MODEL)
    h = x.reshape(S, D)
    pos = positions.reshape(S, 1)

    qt, k, vt = _mla_proj(h, pos, g_mix[0], mla_w_a[0], mla_g_q[0], mla_g_kv[0],
                         mla_w_uq[0], mla_w_ukv[0])
    o = _attention(qt, k, vt)
    w_up_all = ffn_w_up.astype(BF16)
    w_down_all = ffn_w_down.astype(BF16)
    h, u = _attn_out_ffn_s5_in(h, o, mla_w_o[0], g_ffn[0], w_up_all, ffn_conv_w[0],
                               ffn_conv_b[0], w_down_all, g_mix[1], ssm_w_in[0])

    tabs = _s5_tables(ssm_lambda_re[0], ssm_lambda_im[0], ssm_log_dt[0],
                      ssm_b_re[0], ssm_b_im[0], ssm_c_re[0], ssm_c_im[0])
    y = _s5_core(u, *tabs)
    h = _s5_glu_ffn_final(h, y, u, ssm_d[0], ssm_w_glu[0], g_ffn[1], w_up_all,
                          ffn_conv_w[1], ffn_conv_b[1], w_down_all, g_final)
    return h.reshape(B, S, D)
```
